```python
import jax
import jax.numpy as jnp
from jax import lax
import numpy as np

D_MODEL = 1024
BATCH = 2
SEQ = 8192
DEPTH = 4

GRID_W = 64
CTX_LEN = 256
CHUNK = 128
Q_BLOCK = 128
ROPE_THETA = 10000.0
EPS = 1e-6

A_GROUPS = 4
A_GROUP_DIM = 128
A_WIDTH = A_GROUPS * A_GROUP_DIM
B_HEADS = 4
B_KV_HEADS = 2
B_HEAD_DIM = 128
B_Q_WIDTH = B_HEADS * B_HEAD_DIM
B_KV_WIDTH = B_KV_HEADS * B_HEAD_DIM
AB_IN_WIDTH = 2 * A_WIDTH + B_Q_WIDTH + 2 * B_KV_WIDTH
AB_OUT_WIDTH = A_WIDTH + B_Q_WIDTH
C_HEADS = 8
C_Q_RANK = 384
C_KV_RANK = 256
C_NOPE = 128
C_ROPE = 64
C_V = 128
D_FF = 2816
N_EXPERTS = 8
TOP_K = 2
D_FF_EXPERT = 3584

kernel_name = 'hybrid_gmlp_gqa_mla_moe_diffusion_trunk'


def rms_norm(x, g):
    xf = x.astype(jnp.float32)
    y = xf * lax.rsqrt(jnp.mean(xf * xf, axis=-1, keepdims=True) + EPS)
    return (y * g.astype(jnp.float32)).astype(x.dtype)


def modulate(x, g, shift, scale):
    return rms_norm(x, g) * (1 + scale) + shift


def split_heads(t, n_heads, head_dim):
    return t.reshape(t.shape[:2] + (n_heads, head_dim))


def axial_rope_table(rows, rot_dim):
    n = rows * GRID_W
    row_idx = jnp.repeat(jnp.arange(rows, dtype=jnp.float32), GRID_W)
    col_idx = jnp.tile(jnp.arange(GRID_W, dtype=jnp.float32), rows)
    n_freq = rot_dim // 4
    inv_freq = ROPE_THETA ** (-jnp.arange(n_freq, dtype=jnp.float32) / n_freq)
    ang = jnp.stack([row_idx[:, None] * inv_freq, col_idx[:, None] * inv_freq], axis=1)
    ang = jnp.broadcast_to(ang[:, :, None, :], (n, 2, 2, n_freq)).reshape(n, rot_dim)
    return jnp.cos(ang), jnp.sin(ang)


def apply_rope(x, cos, sin):
    d = x.shape[-1]
    xs = x.reshape(x.shape[:-1] + (2, 2, d // 4))
    rot = jnp.stack([-xs[..., 1, :], xs[..., 0, :]], axis=-2).reshape(x.shape)
    out = x.astype(jnp.float32) * cos[None, :, None, :] + rot.astype(jnp.float32) * sin[None, :, None, :]
    return out.astype(x.dtype)


def block_attention(q, k, v, scale):
    bsz, s, kh, g, dk = q.shape
    nb = s // Q_BLOCK
    qb = jnp.moveaxis(q.reshape(bsz, nb, Q_BLOCK, kh, g, dk), 1, 0)

    def attend(q_blk):
        sc = jnp.einsum('bqhgd,bthd->bhgqt', q_blk, k, preferred_element_type=jnp.float32) * scale
        p = jax.nn.softmax(sc, axis=-1).astype(v.dtype)
        return jnp.einsum('bhgqt,bthd->bqhgd', p, v)

    o = lax.map(attend, qb)
    return jnp.moveaxis(o, 0, 1).reshape(bsz, s, kh, g, v.shape[-1])


def chunk_spatial_gating(u, v, norm_g, ws, bs):
    u = jax.nn.gelu(u)
    v = jax.nn.gelu(v)
    bsz, n, _ = v.shape
    vg = rms_norm(v.reshape(bsz, n, A_GROUPS, A_GROUP_DIM), norm_g.reshape(A_GROUPS, A_GROUP_DIM))
    vc = vg.reshape(bsz, n // CHUNK, CHUNK, A_GROUPS, A_GROUP_DIM)
    mixed = jnp.einsum('gij,bnjgc->bnigc', ws, vc) + bs.T[:, :, None]
    return u * mixed.reshape(bsz, n, A_WIDTH)


def mixer_gmlp_gqa(h_lat, h_ctx, w_in, a_norm_g, a_ws, a_bs, q_norm_g, k_norm_g, w_out, cos, sin, need_ctx):
    bsz, n, _ = h_lat.shape
    cuts = [A_WIDTH, 2 * A_WIDTH, 2 * A_WIDTH + B_Q_WIDTH, 2 * A_WIDTH + B_Q_WIDTH + B_KV_WIDTH]
    group = B_HEADS // B_KV_HEADS
    scale = B_HEAD_DIM ** -0.5
    u_l, v_l, q_l, k_l, val_l = jnp.split(h_lat @ w_in, cuts, axis=-1)
    if need_ctx:
        u_c, v_c, q_c, k_c, val_c = jnp.split(h_ctx @ w_in, cuts, axis=-1)
    else:
        k_c, val_c = jnp.split(h_ctx @ w_in[:, cuts[2]:], [B_KV_WIDTH], axis=-1)
    k_c = rms_norm(split_heads(k_c, B_KV_HEADS, B_HEAD_DIM), k_norm_g)
    val_c = split_heads(val_c, B_KV_HEADS, B_HEAD_DIM)
    k_l = apply_rope(rms_norm(split_heads(k_l, B_KV_HEADS, B_HEAD_DIM), k_norm_g), cos, sin)
    q_l = apply_rope(rms_norm(split_heads(q_l, B_HEADS, B_HEAD_DIM), q_norm_g), cos, sin)
    keys = jnp.concatenate([k_l, k_c], axis=1)
    vals = jnp.concatenate([split_heads(val_l, B_KV_HEADS, B_HEAD_DIM), val_c], axis=1)
    o_l = block_attention(q_l.reshape(bsz, n, B_KV_HEADS, group, B_HEAD_DIM), keys, vals, scale)
    a_l = chunk_spatial_gating(u_l, v_l, a_norm_g, a_ws, a_bs)
    y_l = jnp.concatenate([a_l, o_l.reshape(bsz, n, B_Q_WIDTH)], axis=-1) @ w_out
    if not need_ctx:
        return y_l, None
    nc = h_ctx.shape[1]
    q_c = rms_norm(split_heads(q_c, B_HEADS, B_HEAD_DIM), q_norm_g)
    o_c = block_attention(q_c.reshape(bsz, nc, B_KV_HEADS, group, B_HEAD_DIM), k_c, val_c, scale)
    a_c = chunk_spatial_gating(u_c, v_c, a_norm_g, a_ws, a_bs)
    y_c = jnp.concatenate([a_c, o_c.reshape(bsz, nc, B_Q_WIDTH)], axis=-1) @ w_out
    return y_l, y_c


def mla_keys_values(h, w_dkv, kv_norm_g, w_ukv, cos, sin):
    bsz, n, _ = h.shape
    kv_in = h @ w_dkv
    c_kv = rms_norm(kv_in[..., :C_KV_RANK], kv_norm_g)
    k_rope = kv_in[..., C_KV_RANK:][:, :, None, :]
    if cos is not None:
        k_rope = apply_rope(k_rope, cos, sin)
    kv = (c_kv @ w_ukv).reshape(bsz, n, C_HEADS, C_NOPE + C_V)
    k_nope, v = kv[..., :C_NOPE], kv[..., C_NOPE:]
    k = jnp.concatenate([k_nope, jnp.broadcast_to(k_rope, (bsz, n, C_HEADS, C_ROPE))], axis=-1)
    return k, v


def mla_queries(h, w_dq, q_norm_g, w_uq, cos, sin):
    bsz, n, _ = h.shape
    c_q = rms_norm(h @ w_dq, q_norm_g)
    q = (c_q @ w_uq).reshape(bsz, n, C_HEADS, C_NOPE + C_ROPE)
    q_nope, q_rope = q[..., :C_NOPE], q[..., C_NOPE:]
    if cos is not None:
        q_rope = apply_rope(q_rope, cos, sin)
    return jnp.concatenate([q_nope, q_rope], axis=-1)[:, :, :, None, :]


def mixer_mla(h_lat, h_ctx, w_dq, q_norm_g, w_uq, w_dkv, kv_norm_g, w_ukv, w_o, cos, sin, need_ctx):
    bsz, n, _ = h_lat.shape
    scale = (C_NOPE + C_ROPE) ** -0.5
    k_c, v_c = mla_keys_values(h_ctx, w_dkv, kv_norm_g, w_ukv, None, None)
    k_l, v_l = mla_keys_values(h_lat, w_dkv, kv_norm_g, w_ukv, cos, sin)
    q_l = mla_queries(h_lat, w_dq, q_norm_g, w_uq, cos, sin)
    o_l = block_attention(q_l, jnp.concatenate([k_l, k_c], axis=1), jnp.concatenate([v_l, v_c], axis=1), scale)
    y_l = o_l.reshape(bsz, n, C_HEADS * C_V) @ w_o
    if not need_ctx:
        return y_l, None
    nc = h_ctx.shape[1]
    q_c = mla_queries(h_ctx, w_dq, q_norm_g, w_uq, None, None)
    o_c = block_attention(q_c, k_c, v_c, scale)
    y_c = o_c.reshape(bsz, nc, C_HEADS * C_V) @ w_o
    return y_l, y_c


def swiglu(h, w_in, w_out):
    g, u = jnp.split(h @ w_in, 2, axis=-1)
    return (jax.nn.silu(g) * u) @ w_out


def moe_swiglu(h, router, w_in, w_out):
    logits = jnp.einsum('bnd,de->bne', h, router).astype(jnp.float32)
    top_logit, top_idx = lax.top_k(logits, TOP_K)
    top_w = jax.nn.softmax(top_logit, axis=-1)
    gates = jnp.sum(jax.nn.one_hot(top_idx, N_EXPERTS, dtype=jnp.float32) * top_w[..., None], axis=-2)
    out = jnp.zeros_like(h)
    for e in range(N_EXPERTS):
        out = out + gates[..., e:e + 1].astype(h.dtype) * swiglu(h, w_in[e], w_out[e])
    return out


def setup_inputs(seed: int = 0) -> dict:
    key = jax.random.key(seed)
    keys = iter(jax.random.split(key, 40))
    f32 = jnp.float32

    def normal(shape, scale):
        return jax.random.normal(next(keys), shape, f32) * scale

    def gain(shape):
        return 1.0 + normal(shape, 0.02)

    ne = (DEPTH + 1) // 2
    no = DEPTH // 2
    return {
        'x': normal((BATCH, SEQ, D_MODEL), 1.0),
        'c': normal((BATCH, D_MODEL), 1.0),
        'ctx': normal((BATCH, CTX_LEN, D_MODEL), 1.0),
        'c_ctx': normal((D_MODEL,), 1.0),
        'ada_w': normal((DEPTH, D_MODEL, 6 * D_MODEL), 0.5 * D_MODEL ** -0.5),
        'ada_b': normal((DEPTH, 6 * D_MODEL), 0.02),
        'g_pre_mix': gain((DEPTH, D_MODEL)),
        'g_post_mix': gain((DEPTH, D_MODEL)),
        'g_pre_ffn': gain((DEPTH, D_MODEL)),
        'g_post_ffn': gain((DEPTH, D_MODEL)),
        'ab_w_in': normal((ne, D_MODEL, AB_IN_WIDTH), D_MODEL ** -0.5),
        'a_norm_g': gain((ne, A_WIDTH)),
        'a_ws': normal((ne, A_GROUPS, CHUNK, CHUNK), CHUNK ** -0.5),
        'a_bs': normal((ne, A_GROUPS, CHUNK), 0.02),
        'b_qnorm_g': gain((ne, B_HEAD_DIM)),
        'b_knorm_g': gain((ne, B_HEAD_DIM)),
        'ab_w_out': normal((ne, AB_OUT_WIDTH, D_MODEL), AB_OUT_WIDTH ** -0.5),
        'ffn_w_in': normal((ne, D_MODEL, 2 * D_FF), D_MODEL ** -0.5),
        'ffn_w_out': normal((ne, D_FF, D_MODEL), D_FF ** -0.5),
        'mla_w_dq': normal((no, D_MODEL, C_Q_RANK), D_MODEL ** -0.5),
        'mla_qnorm_g': gain((no, C_Q_RANK)),
        'mla_w_uq': normal((no, C_Q_RANK, C_HEADS * (C_NOPE + C_ROPE)), C_Q_RANK ** -0.5),
        'mla_w_dkv': normal((no, D_MODEL, C_KV_RANK + C_ROPE), D_MODEL ** -0.5),
        'mla_kvnorm_g': gain((no, C_KV_RANK)),
        'mla_w_ukv': normal((no, C_KV_RANK, C_HEADS * (C_NOPE + C_V)), C_KV_RANK ** -0.5),
        'mla_w_o': normal((no, C_HEADS * C_V, D_MODEL), (C_HEADS * C_V) ** -0.5),
        'moe_router': normal((no, D_MODEL, N_EXPERTS), D_MODEL ** -0.5),
        'moe_w_in': normal((no, N_EXPERTS, D_MODEL, 2 * D_FF_EXPERT), D_MODEL ** -0.5),
        'moe_w_out': normal((no, N_EXPERTS, D_FF_EXPERT, D_MODEL), D_FF_EXPERT ** -0.5),
    }


def reference(x, c, ctx, c_ctx, ada_w, ada_b, g_pre_mix, g_post_mix, g_pre_ffn, g_post_ffn,
              ab_w_in, a_norm_g, a_ws, a_bs, b_qnorm_g, b_knorm_g, ab_w_out, ffn_w_in, ffn_w_out,
              mla_w_dq, mla_qnorm_g, mla_w_uq, mla_w_dkv, mla_kvnorm_g, mla_w_ukv, mla_w_o,
              moe_router, moe_w_in, moe_w_out):
    n_ctx = ctx.shape[1]
    rows = x.shape[1] // GRID_W
    cos_b, sin_b = axial_rope_table(rows, B_HEAD_DIM)
    cos_c, sin_c = axial_rope_table(rows, C_ROPE)
    act_lat = jax.nn.silu(c)
    act_ctx = jax.nn.silu(c_ctx)[None, :]
    for layer in range(DEPTH):
        need_ctx = layer < DEPTH - 1
        i = layer // 2
        mod_lat = jnp.split((act_lat @ ada_w[layer] + ada_b[layer])[:, None, :], 6, axis=-1)
        mod_ctx = jnp.split((act_ctx @ ada_w[layer] + ada_b[layer])[:, None, :], 6, axis=-1)
        h_lat = modulate(x, g_pre_mix[layer], mod_lat[0], mod_lat[1])
        h_ctx = modulate(ctx, g_pre_mix[layer], mod_ctx[0], mod_ctx[1])
        if layer % 2 == 0:
            y_lat, y_ctx = mixer_gmlp_gqa(h_lat, h_ctx, ab_w_in[i], a_norm_g[i], a_ws[i], a_bs[i],
                                          b_qnorm_g[i], b_knorm_g[i], ab_w_out[i], cos_b, sin_b, need_ctx)
        else:
            y_lat, y_ctx = mixer_mla(h_lat, h_ctx, mla_w_dq[i], mla_qnorm_g[i], mla_w_uq[i], mla_w_dkv[i],
                                     mla_kvnorm_g[i], mla_w_ukv[i], mla_w_o[i], cos_c, sin_c, need_ctx)
        x = x + mod_lat[2] * rms_norm(y_lat, g_post_mix[layer])
        f_lat = modulate(x, g_pre_ffn[layer], mod_lat[3], mod_lat[4])
        if need_ctx:
            ctx = ctx + mod_ctx[2] * rms_norm(y_ctx, g_post_mix[layer])
            f_ctx = modulate(ctx, g_pre_ffn[layer], mod_ctx[3], mod_ctx[4])
            f_in = jnp.concatenate([f_ctx, f_lat], axis=1)
        else:
            f_in = f_lat
        if layer % 2 == 0:
            f_out = swiglu(f_in, ffn_w_in[i], ffn_w_out[i])
        else:
            f_out = moe_swiglu(f_in, moe_router[i], moe_w_in[i], moe_w_out[i])
        f_out = rms_norm(f_out, g_post_ffn[layer])
        if need_ctx:
            ctx = ctx + mod_ctx[5] * f_out[:, :n_ctx]
            x = x + mod_lat[5] * f_out[:, n_ctx:]
        else:
            x = x + mod_lat[5] * f_out
    return x
```

```python
import functools
import math

import jax
import jax.numpy as jnp
from jax import lax
from jax.experimental import pallas as pl
from jax.experimental.pallas import tpu as pltpu

F32 = jnp.float32
BF16 = jnp.bfloat16

GRID_W = 64
ROPE_THETA = 10000.0
EPS = 1e-6
CHUNK = 128
A_GROUPS = 4
A_GROUP_DIM = 128
A_WIDTH = A_GROUPS * A_GROUP_DIM
B_HEADS = 4
B_KV_HEADS = 2
B_HEAD_DIM = 128
C_HEADS = 8
C_KV_RANK = 256
C_NOPE = 128
C_ROPE = 64
C_V = 128
N_EXPERTS = 8
LANES = 128

VMEM_LIMIT_BYTES = 56 * 1024 * 1024
TOKEN_TILE = 512
ROW_DMA_TILE = 256
FLASH_KV_CHUNK = 512
EXPERT_TILE = 512
EXPERT_FF_TILE = 512
FFN_CHUNK = 256
LOG2E = math.log2(math.e)


def _cparams(*sem):
    return pltpu.CompilerParams(dimension_semantics=sem, vmem_limit_bytes=VMEM_LIMIT_BYTES)


def _dot(a, b):
    return jnp.dot(a, b, preferred_element_type=F32)


def _dot_nt(a, b):
    return lax.dot_general(a, b, (((1,), (1,)), ((), ())), preferred_element_type=F32)


def _rms(x, g):
    return x * lax.rsqrt(jnp.mean(x * x, axis=-1, keepdims=True) + EPS) * g


def _gelu(x):
    c = math.sqrt(2.0 / math.pi)
    return x * (0.5 * (1.0 + jnp.tanh(c * (x + 0.044715 * (x * x * x)))))


def _silu(x):
    return x * (1.0 / (1.0 + jnp.exp(-x)))


def _rope(x, cos, sin_lo, sin_hi, quarter):
    return (x * cos + pltpu.roll(x, LANES - quarter, 1) * sin_lo
            + pltpu.roll(x, quarter, 1) * sin_hi)


def _modulated(x, g, mod, shift_row, scale_row):
    return (_rms(x, g) * (1.0 + mod[scale_row:scale_row + 1, :])
            + mod[shift_row:shift_row + 1, :])


def _adaln_kernel(c_ref, w_ref, b_ref, o_ref):
    act = _silu(c_ref[...]).astype(BF16)
    o_ref[0, 0] = _dot(act, w_ref[0].astype(BF16)) + b_ref[0, 0]


def _adaln(cpad, ada_w, ada_b):
    depth, d, _ = ada_w.shape
    rows = cpad.shape[0]
    out = pl.pallas_call(
        _adaln_kernel,
        out_shape=jax.ShapeDtypeStruct((depth, 6, rows, d), F32),
        grid=(depth, 6),
        in_specs=[
            pl.BlockSpec((rows, d), lambda l, j: (0, 0)),
            pl.BlockSpec((1, d, d), lambda l, j: (l, 0, j)),
            pl.BlockSpec((1, 1, 1, d), lambda l, j: (l, j, 0, 0)),
        ],
        out_specs=pl.BlockSpec((1, 1, rows, d), lambda l, j: (l, j, 0, 0)),
        compiler_params=_cparams("arbitrary", "arbitrary"),
        name="adaln",
    )(cpad, ada_w, ada_b.reshape(depth, 6, 1, d))
    return jnp.transpose(out, (0, 2, 1, 3))


def _tok_spec(tm, width):
    return pl.BlockSpec((1, tm, width), lambda b, t: (b, t, 0))


def _mod_spec(n_lat_tiles, ctx_row, d):
    return pl.BlockSpec((1, 6, d), lambda b, t: (jnp.where(t >= n_lat_tiles, ctx_row, b), 0, 0))


def _const_spec(shape):
    nd = len(shape)
    return pl.BlockSpec(shape, lambda b, t: (0,) * nd)


def _table_spec(tm):
    return pl.BlockSpec((tm, LANES), lambda b, t: (t, 0))


def _ab_in_kernel(x_ref, mod_ref, gpre_ref, w_ref, ang_ref, qg_ref, kg_ref, cos_ref, slo_ref, shi_ref,
                  u_ref, vg_ref, q_ref, k_ref, v_ref, *, qscale):
    hb = _modulated(x_ref[0], gpre_ref[...], mod_ref[0], 0, 1).astype(BF16)
    cos = cos_ref[...]
    slo = slo_ref[...]
    shi = shi_ref[...]
    quarter = B_HEAD_DIM // 4
    u_ref[0] = _gelu(_dot(hb, w_ref[:, 0:A_WIDTH])).astype(BF16)
    v = _gelu(_dot(hb, w_ref[:, A_WIDTH:2 * A_WIDTH]))
    for g in range(A_GROUPS):
        sl = slice(g * A_GROUP_DIM, (g + 1) * A_GROUP_DIM)
        vg_ref[0, :, sl] = _rms(v[:, sl], ang_ref[:, sl]).astype(BF16)
    c0 = 2 * A_WIDTH
    q = _dot(hb, w_ref[:, c0:c0 + B_HEADS * B_HEAD_DIM])
    for h in range(B_HEADS):
        sl = slice(h * B_HEAD_DIM, (h + 1) * B_HEAD_DIM)
        qr = _rope(_rms(q[:, sl], qg_ref[...]), cos, slo, shi, quarter)
        q_ref[0, :, sl] = (qr * qscale).astype(BF16)
    c1 = c0 + B_HEADS * B_HEAD_DIM
    k = _dot(hb, w_ref[:, c1:c1 + B_KV_HEADS * B_HEAD_DIM])
    for h in range(B_KV_HEADS):
        sl = slice(h * B_HEAD_DIM, (h + 1) * B_HEAD_DIM)
        k_ref[0, :, sl] = _rope(_rms(k[:, sl], kg_ref[...]), cos, slo, shi, quarter).astype(BF16)
    c2 = c1 + B_KV_HEADS * B_HEAD_DIM
    v_ref[0] = _dot(hb, w_ref[:, c2:c2 + B_KV_HEADS * B_HEAD_DIM]).astype(BF16)


def _ab_in(X, mod, gpre, w_in, a_norm_g, qg, kg, tables, n_lat):
    B, T, D = X.shape
    tm = TOKEN_TILE
    qw = B_HEADS * B_HEAD_DIM
    kw = B_KV_HEADS * B_HEAD_DIM
    qscale = (B_HEAD_DIM ** -0.5) * LOG2E
    outs = pl.pallas_call(
        functools.partial(_ab_in_kernel, qscale=qscale),
        out_shape=[
            jax.ShapeDtypeStruct((B, T, A_WIDTH), BF16),
            jax.ShapeDtypeStruct((B, T, A_WIDTH), BF16),
            jax.ShapeDtypeStruct((B, T, qw), BF16),
            jax.ShapeDtypeStruct((B, T, kw), BF16),
            jax.ShapeDtypeStruct((B, T, kw), BF16),
        ],
        grid=(B, pl.cdiv(T, tm)),
        in_specs=[
            _tok_spec(tm, D),
            _mod_spec(n_lat // tm, B, D),
            _const_spec((1, D)),
            _const_spec(w_in.shape),
            _const_spec((1, A_WIDTH)),
            _const_spec((1, B_HEAD_DIM)),
            _const_spec((1, B_HEAD_DIM)),
            _table_spec(tm), _table_spec(tm), _table_spec(tm),
        ],
        out_specs=[_tok_spec(tm, A_WIDTH), _tok_spec(tm, A_WIDTH), _tok_spec(tm, qw),
                   _tok_spec(tm, kw), _tok_spec(tm, kw)],
        compiler_params=_cparams("arbitrary", "arbitrary"),
        name="ab_in",
    )(X, mod, gpre.reshape(1, D), w_in, a_norm_g.reshape(1, A_WIDTH), qg.reshape(1, -1), kg.reshape(1, -1),
      *tables)
    return outs


def _flash_kernel(q_ref, k_ref, v_ref, o_ref, m_ref, l_ref, acc_ref, *, group, dk, dv, kv_len, tk):
    tq = q_ref.shape[1]
    if group == 1:
        q = q_ref[0]
    else:
        q = jnp.concatenate([q_ref[0, :, g * dk:(g + 1) * dk] for g in range(group)], axis=0)
    m_ref[...] = jnp.full(m_ref.shape, -jnp.inf, F32)
    l_ref[...] = jnp.zeros(l_ref.shape, F32)
    acc_ref[...] = jnp.zeros(acc_ref.shape, F32)

    def step(start, size):
        k = k_ref[0, pl.ds(start, size), :]
        v = v_ref[0, pl.ds(start, size), :]
        s = _dot_nt(q, k)
        m_prev = m_ref[...]
        m_new = jnp.maximum(m_prev, jnp.max(s, axis=-1, keepdims=True))
        alpha = jnp.exp2(m_prev - m_new)
        p = jnp.exp2(s - m_new)
        l_ref[...] = alpha * l_ref[...] + jnp.sum(p, axis=-1, keepdims=True)
        acc_ref[...] = alpha * acc_ref[...] + _dot(p.astype(BF16), v)
        m_ref[...] = m_new

    n_full = kv_len // tk
    if n_full > 0:
        def body(c, carry):
            step(pl.multiple_of(c * tk, tk), tk)
            return carry
        lax.fori_loop(0, n_full, body, 0)
    tail = kv_len - n_full * tk
    if tail > 0:
        step(n_full * tk, tail)
    out = acc_ref[...] * (1.0 / l_ref[...])
    for g in range(group):
        o_ref[0, :, g * dv:(g + 1) * dv] = out[g * tq:(g + 1) * tq].astype(o_ref.dtype)


def _flash(q, k, v, *, n_kv_heads, group, dk, dv, tq, q_row0, n_q_rows, kv_row0, kv_len, name):
    B = q.shape[0]
    q_blk0 = q_row0 // tq
    kv_blk0 = kv_row0 // kv_len
    tk = min(FLASH_KV_CHUNK, kv_len)
    return pl.pallas_call(
        functools.partial(_flash_kernel, group=group, dk=dk, dv=dv, kv_len=kv_len, tk=tk),
        out_shape=jax.ShapeDtypeStruct((B, n_q_rows, n_kv_heads * group * dv), BF16),
        grid=(B, n_kv_heads, n_q_rows // tq),
        in_specs=[
            pl.BlockSpec((1, tq, group * dk), lambda b, h, i: (b, q_blk0 + i, h)),
            pl.BlockSpec((1, kv_len, dk), lambda b, h, i: (b, kv_blk0, h)),
            pl.BlockSpec((1, kv_len, dv), lambda b, h, i: (b, kv_blk0, h)),
        ],
        out_specs=pl.BlockSpec((1, tq, group * dv), lambda b, h, i: (b, i, h)),
        scratch_shapes=[
            pltpu.VMEM((group * tq, 1), F32),
            pltpu.VMEM((group * tq, 1), F32),
            pltpu.VMEM((group * tq, dv), F32),
        ],
        compiler_params=_cparams("arbitrary", "arbitrary", "arbitrary"),
        name=name,
    )(q, k, v)


def _attention(q, k, v, *, n_lat, n_ctx, need_ctx, name, **kw):
    T = n_lat + n_ctx
    o_lat = _flash(q, k, v, q_row0=0, n_q_rows=n_lat, kv_row0=0, kv_len=T, name=name + "_lat", **kw)
    if not need_ctx:
        return o_lat
    kw = dict(kw, tq=min(kw["tq"], n_ctx))
    o_ctx = _flash(q, k, v, q_row0=n_lat, n_q_rows=n_ctx, kv_row0=n_lat, kv_len=n_ctx,
                   name=name + "_ctx", **kw)
    return jnp.concatenate([o_lat, o_ctx], axis=1)


def _post_mix(y, x, mod, gpost, gpre):
    xn = x + mod[2:3, :] * _rms(y, gpost)
    return xn, _modulated(xn, gpre, mod, 3, 4)


def _ab_out_kernel(u_ref, vg_ref, o_ref, ws_ref, bias_ref, wout_ref, x_ref, mod_ref, gpost_ref, gpre_ref,
                   xo_ref, f_ref):
    tm = u_ref.shape[1]
    y = _dot(o_ref[0], wout_ref[A_WIDTH:, :])
    gated = []
    for c in range(tm // CHUNK):
        rows = slice(c * CHUNK, (c + 1) * CHUNK)
        mixed = jnp.concatenate(
            [_dot(ws_ref[g], vg_ref[0, rows, g * A_GROUP_DIM:(g + 1) * A_GROUP_DIM]) for g in range(A_GROUPS)],
            axis=1) + bias_ref[...]
        gated.append((u_ref[0, rows, :].astype(F32) * mixed).astype(BF16))
    y = y + _dot(jnp.concatenate(gated, axis=0), wout_ref[:A_WIDTH, :])
    xn, f = _post_mix(y, x_ref[0], mod_ref[0], gpost_ref[...], gpre_ref[...])
    xo_ref[0] = xn
    f_ref[0] = f.astype(f_ref.dtype)


def _ab_out(u, vg, o, ws, bias, w_out, X, mod, gpost, gpre, n_lat):
    B, T, D = X.shape
    tm = TOKEN_TILE
    return pl.pallas_call(
        _ab_out_kernel,
        out_shape=[jax.ShapeDtypeStruct((B, T, D), F32), jax.ShapeDtypeStruct((B, T, D), BF16)],
        grid=(B, pl.cdiv(T, tm)),
        in_specs=[
            _tok_spec(tm, A_WIDTH), _tok_spec(tm, A_WIDTH), _tok_spec(tm, o.shape[2]),
            _const_spec(ws.shape), _const_spec(bias.shape), _const_spec(w_out.shape),
            _tok_spec(tm, D), _mod_spec(n_lat // tm, B, D), _const_spec((1, D)), _const_spec((1, D)),
        ],
        out_specs=[_tok_spec(tm, D), _tok_spec(tm, D)],
        compiler_params=_cparams("arbitrary", "arbitrary"),
        name="ab_out",
    )(u, vg, o, ws, bias, w_out, X, mod, gpost.reshape(1, D), gpre.reshape(1, D))


def _ffn_kernel(f_ref, wg_ref, wu_ref, wo_ref, x_ref, mod_ref, gpost_ref, xo_ref, acc_ref):
    acc_ref[...] = jnp.zeros(acc_ref.shape, F32)

    def body(c, carry):
        fb = f_ref[0]
        h = (_silu(_dot(fb, wg_ref[c])) * _dot(fb, wu_ref[c])).astype(BF16)
        acc_ref[...] += _dot(h, wo_ref[c])
        return carry

    lax.fori_loop(0, wg_ref.shape[0], body, 0)
    xo_ref[0] = x_ref[0] + mod_ref[0][5:6, :] * _rms(acc_ref[...], gpost_ref[...])


def _ffn(f, wg, wu, wo, X, mod, gpost, n_lat):
    B, T, D = X.shape
    tm = TOKEN_TILE
    return pl.pallas_call(
        _ffn_kernel,
        out_shape=jax.ShapeDtypeStruct((B, T, D), F32),
        grid=(B, pl.cdiv(T, tm)),
        in_specs=[
            _tok_spec(tm, D), _const_spec(wg.shape), _const_spec(wu.shape), _const_spec(wo.shape),
            _tok_spec(tm, D), _mod_spec(n_lat // tm, B, D), _const_spec((1, D)),
        ],
        out_specs=_tok_spec(tm, D),
        scratch_shapes=[pltpu.VMEM((tm, D), F32)],
        compiler_params=_cparams("arbitrary", "arbitrary"),
        name="ffn",
    )(f, wg, wu, wo, X, mod, gpost.reshape(1, D))


def _mla_in_kernel(x_ref, mod_ref, gpre_ref, wdq_ref, qng_ref, wuq_ref, wdkv_ref, kvng_ref, wuk_ref, wuv_ref,
                   cos_ref, slo_ref, shi_ref, q_ref, k_ref, v_ref, *, qscale):
    hb = _modulated(x_ref[0], gpre_ref[...], mod_ref[0], 0, 1).astype(BF16)
    cos = cos_ref[...]
    slo = slo_ref[...]
    shi = shi_ref[...]
    quarter = C_ROPE // 4
    hw = 2 * LANES
    cq = _rms(_dot(hb, wdq_ref[...]), qng_ref[...]).astype(BF16)
    qf = _dot(cq, wuq_ref[...])
    for h in range(C_HEADS):
        q_ref[0, :, h * hw:h * hw + LANES] = (qf[:, h * hw:h * hw + LANES] * qscale).astype(BF16)
        qr = _rope(qf[:, h * hw + LANES:(h + 1) * hw], cos, slo, shi, quarter)
        q_ref[0, :, h * hw + LANES:(h + 1) * hw] = (qr * qscale).astype(BF16)
    kvin = _dot(hb, wdkv_ref[...])
    ckv = _rms(kvin[:, :C_KV_RANK], kvng_ref[...]).astype(BF16)
    krope = _rope(kvin[:, C_KV_RANK:], cos, slo, shi, quarter).astype(BF16)
    kn = _dot(ckv, wuk_ref[...])
    for h in range(C_HEADS):
        k_ref[0, :, h * hw:h * hw + LANES] = kn[:, h * C_NOPE:(h + 1) * C_NOPE].astype(BF16)
        k_ref[0, :, h * hw + LANES:(h + 1) * hw] = krope
    v_ref[0] = _dot(ckv, wuv_ref[...]).astype(BF16)


def _mla_in(X, mod, gpre, wdq, qng, wuq, wdkv, kvng, wuk, wuv, tables, n_lat):
    B, T, D = X.shape
    tm = TOKEN_TILE
    qscale = ((C_NOPE + C_ROPE) ** -0.5) * LOG2E
    hw = 2 * LANES
    return pl.pallas_call(
        functools.partial(_mla_in_kernel, qscale=qscale),
        out_shape=[
            jax.ShapeDtypeStruct((B, T, C_HEADS * hw), BF16),
            jax.ShapeDtypeStruct((B, T, C_HEADS * hw), BF16),
            jax.ShapeDtypeStruct((B, T, C_HEADS * C_V), BF16),
        ],
        grid=(B, pl.cdiv(T, tm)),
        in_specs=[
            _tok_spec(tm, D), _mod_spec(n_lat // tm, B, D), _const_spec((1, D)),
            _const_spec(wdq.shape), _const_spec((1, qng.shape[0])), _const_spec(wuq.shape),
            _const_spec(wdkv.shape), _const_spec((1, kvng.shape[0])), _const_spec(wuk.shape),
            _const_spec(wuv.shape),
            _table_spec(tm), _table_spec(tm), _table_spec(tm),
        ],
        out_specs=[_tok_spec(tm, C_HEADS * hw), _tok_spec(tm, C_HEADS * hw), _tok_spec(tm, C_HEADS * C_V)],
        compiler_params=_cparams("arbitrary", "arbitrary"),
        name="mla_in",
    )(X, mod, gpre.reshape(1, D), wdq, qng.reshape(1, -1), wuq, wdkv, kvng.reshape(1, -1), wuk, wuv, *tables)


def _split_bf16(x):
    hi = x.astype(BF16)
    lo = (x - hi.astype(F32)).astype(BF16)
    return hi, lo


def _mla_out_kernel(o_ref, wo_ref, x_ref, mod_ref, gpost_ref, gpre_ref, rt_ref, xo_ref, f_ref, lg_ref):
    y = _dot(o_ref[0], wo_ref[...])
    xn, f = _post_mix(y, x_ref[0], mod_ref[0], gpost_ref[...], gpre_ref[...])
    xo_ref[0] = xn
    f_ref[0] = f
    f_hi, f_lo = _split_bf16(f)
    r_hi, r_lo = _split_bf16(rt_ref[...])
    lg_ref[0] = _dot_nt(r_hi, f_hi) + (_dot_nt(r_hi, f_lo) + _dot_nt(r_lo, f_hi))


def _mla_out(o, w_o, X, mod, gpost, gpre, router_t, n_lat, n_rows):
    B, _, D = X.shape
    tm = TOKEN_TILE
    ne = router_t.shape[0]
    return pl.pallas_call(
        _mla_out_kernel,
        out_shape=[jax.ShapeDtypeStruct((B, n_rows, D), F32), jax.ShapeDtypeStruct((B, n_rows, D), F32),
                   jax.ShapeDtypeStruct((B, ne, n_rows), F32)],
        grid=(B, pl.cdiv(n_rows, tm)),
        in_specs=[
            _tok_spec(tm, D), _const_spec(w_o.shape), _tok_spec(tm, D), _mod_spec(n_lat // tm, B, D),
            _const_spec((1, D)), _const_spec((1, D)), _const_spec(router_t.shape),
        ],
        out_specs=[_tok_spec(tm, D), _tok_spec(tm, D), pl.BlockSpec((1, ne, tm), lambda b, t: (b, 0, t))],
        compiler_params=_cparams("arbitrary", "arbitrary"),
        name="mla_out",
    )(o, w_o, X, mod, gpost.reshape(1, D), gpre.reshape(1, D), router_t)


def _route_kernel(lg_ref, ints_ref, gates_ref, cnt_ref, carry_ref):
    first = jnp.logical_and(pl.program_id(0) == 0, pl.program_id(1) == 0)

    @pl.when(first)
    def _():
        carry_ref[...] = jnp.zeros(carry_ref.shape, F32)

    lg = lg_ref[0]
    ne, tn = lg.shape
    eidx = lax.broadcasted_iota(jnp.int32, (ne, tn), 0)
    m1 = jnp.max(lg, axis=0, keepdims=True)
    i1 = jnp.min(jnp.where(lg == m1, eidx, ne), axis=0, keepdims=True)
    rest = jnp.where(eidx == i1, -jnp.inf, lg)
    m2 = jnp.max(rest, axis=0, keepdims=True)
    i2 = jnp.min(jnp.where(rest == m2, eidx, ne), axis=0, keepdims=True)
    e2 = jnp.exp(m2 - m1)
    den = 1.0 + e2
    w1 = 1.0 / den
    w2 = e2 / den
    member = jnp.logical_or(eidx == i1, eidx == i2)
    before = lax.broadcasted_iota(jnp.int32, (tn, tn), 0) < lax.broadcasted_iota(jnp.int32, (tn, tn), 1)
    cum = _dot(member.astype(BF16), before.astype(BF16)) + carry_ref[:, 0:1]
    r1 = jnp.sum(jnp.where(eidx == i1, cum, 0.0), axis=0, keepdims=True)
    r2 = jnp.sum(jnp.where(eidx == i2, cum, 0.0), axis=0, keepdims=True)
    carry_ref[...] = carry_ref[...] + jnp.sum(member.astype(F32), axis=1, keepdims=True)
    row = lax.broadcasted_iota(jnp.int32, (8, tn), 0)
    ints = jnp.where(row == 0, i1, jnp.where(row == 1, i2, jnp.where(row == 2, r1.astype(jnp.int32),
                                                                      r2.astype(jnp.int32))))
    ints_ref[0] = ints
    gates_ref[0] = jnp.where(row == 0, w1, w2)
    cnt_ref[...] = carry_ref[...]


def _route(logits_t):
    B, ne, n = logits_t.shape
    tn = ROW_DMA_TILE
    return pl.pallas_call(
        _route_kernel,
        out_shape=[jax.ShapeDtypeStruct((B, 8, n), jnp.int32), jax.ShapeDtypeStruct((B, 8, n), F32),
                   jax.ShapeDtypeStruct((ne, LANES), F32)],
        grid=(B, n // tn),
        in_specs=[pl.BlockSpec((1, ne, tn), lambda b, t: (b, 0, t))],
        out_specs=[pl.BlockSpec((1, 8, tn), lambda b, t: (b, 0, t)),
                   pl.BlockSpec((1, 8, tn), lambda b, t: (b, 0, t)),
                   pl.BlockSpec((ne, LANES), lambda b, t: (0, 0))],
        scratch_shapes=[pltpu.VMEM((ne, LANES), F32)],
        compiler_params=_cparams("arbitrary", "arbitrary"),
        name="moe_route",
    )(logits_t)


def _row_copy(src_ref, src_row, dst_ref, dst_row, sem):
    return pltpu.make_async_copy(src_ref.at[pl.ds(src_row, 1), :], dst_ref.at[pl.ds(dst_row, 1), :], sem)


def _dispatch_kernel(pos_ref, f_ref, xs_in_ref, xs_ref, sem):
    del xs_in_ref
    tm = f_ref.shape[1]
    src = f_ref.at[0]

    def issue(r, carry):
        _row_copy(src, r, xs_ref, pos_ref[0, 0, r], sem.at[0]).start()
        _row_copy(src, r, xs_ref, pos_ref[0, 1, r], sem.at[1]).start()
        return carry

    lax.fori_loop(0, tm, issue, 0)

    def drain(r, carry):
        _row_copy(src, r, xs_ref, 0, sem.at[0]).wait()
        _row_copy(src, r, xs_ref, 0, sem.at[1]).wait()
        return carry

    lax.fori_loop(0, tm, drain, 0)


def _dispatch(pos, f, n_sorted):
    B, n, D = f.shape
    tm = ROW_DMA_TILE
    zeros = jnp.zeros((n_sorted, D), f.dtype)
    return pl.pallas_call(
        _dispatch_kernel,
        out_shape=jax.ShapeDtypeStruct((n_sorted, D), f.dtype),
        grid=(B, n // tm),
        in_specs=[
            pl.BlockSpec((1, 2, tm), lambda b, t: (b, 0, t), memory_space=pltpu.SMEM),
            _tok_spec(tm, D),
            pl.BlockSpec(memory_space=pl.ANY),
        ],
        out_specs=pl.BlockSpec(memory_space=pl.ANY),
        scratch_shapes=[pltpu.SemaphoreType.DMA((2,))],
        input_output_aliases={2: 0},
        compiler_params=_cparams("arbitrary", "arbitrary"),
        name="moe_dispatch",
    )(pos, f, zeros)


def _expert_kernel(te_ref, ta_ref, xs_ref, wg_ref, wu_ref, wo_ref, y_ref, acc_ref):
    i = pl.program_id(0)
    j = pl.program_id(1)
    last = pl.num_programs(1) - 1
    active = ta_ref[i] == 1

    @pl.when(jnp.logical_and(active, j == 0))
    def _():
        acc_ref[...] = jnp.zeros(acc_ref.shape, F32)

    @pl.when(active)
    def _():
        xb = xs_ref[...].astype(BF16)
        h = (_silu(_dot(xb, wg_ref[0])) * _dot(xb, wu_ref[0])).astype(BF16)
        acc_ref[...] += _dot(h, wo_ref[0])

    @pl.when(jnp.logical_and(active, j == last))
    def _():
        y_ref[...] = acc_ref[...]

    @pl.when(jnp.logical_and(jnp.logical_not(active), j == last))
    def _():
        y_ref[...] = jnp.zeros(y_ref.shape, F32)


def _experts(tile_expert, tile_active, xs, w_in, w_out):
    n_sorted, D = xs.shape
    tm = EXPERT_TILE
    tf = EXPERT_FF_TILE
    dff = w_out.shape[1]
    nf = dff // tf

    def jj(i, j, ta):
        return jnp.where(ta[i] == 1, j, nf - 1)

    grid_spec = pltpu.PrefetchScalarGridSpec(
        num_scalar_prefetch=2,
        grid=(n_sorted // tm, nf),
        in_specs=[
            pl.BlockSpec((tm, D), lambda i, j, te, ta: (i, 0)),
            pl.BlockSpec((1, D, tf), lambda i, j, te, ta: (te[i], 0, jj(i, j, ta))),
            pl.BlockSpec((1, D, tf), lambda i, j, te, ta: (te[i], 0, nf + jj(i, j, ta))),
            pl.BlockSpec((1, tf, D), lambda i, j, te, ta: (te[i], jj(i, j, ta), 0)),
        ],
        out_specs=pl.BlockSpec((tm, D), lambda i, j, te, ta: (i, 0)),
        scratch_shapes=[pltpu.VMEM((tm, D), F32)],
    )
    return pl.pallas_call(
        _expert_kernel,
        out_shape=jax.ShapeDtypeStruct((n_sorted, D), F32),
        grid_spec=grid_spec,
        compiler_params=_cparams("arbitrary", "arbitrary"),
        name="moe_experts",
    )(tile_expert, tile_active, xs, w_in, w_in, w_out)


def _combine_kernel(pos_ref, ys_ref, g_ref, x_ref, mod_ref, gpost_ref, xo_ref, buf_a, buf_b, sem):
    tm = x_ref.shape[1]

    def issue(r, carry):
        _row_copy(ys_ref, pos_ref[0, 0, r], buf_a, r, sem.at[0]).start()
        _row_copy(ys_ref, pos_ref[0, 1, r], buf_b, r, sem.at[1]).start()
        return carry

    lax.fori_loop(0, tm, issue, 0)

    def drain(r, carry):
        _row_copy(ys_ref, 0, buf_a, r, sem.at[0]).wait()
        _row_copy(ys_ref, 0, buf_b, r, sem.at[1]).wait()
        return carry

    lax.fori_loop(0, tm, drain, 0)
    g = g_ref[0]
    fo = g[:, 0:1] * buf_a[...] + g[:, 1:2] * buf_b[...]
    xo_ref[0] = x_ref[0] + mod_ref[0][5:6, :] * _rms(fo, gpost_ref[...])


def _combine(pos, ys, gates, X, mod, gpost, n_lat, n_rows):
    B, _, D = X.shape
    tm = ROW_DMA_TILE
    return pl.pallas_call(
        _combine_kernel,
        out_shape=jax.ShapeDtypeStruct((B, n_rows, D), F32),
        grid=(B, n_rows // tm),
        in_specs=[
            pl.BlockSpec((1, 2, tm), lambda b, t: (b, 0, t), memory_space=pltpu.SMEM),
            pl.BlockSpec(memory_space=pl.ANY),
            _tok_spec(tm, 2),
            _tok_spec(tm, D),
            _mod_spec(n_lat // tm, B, D),
            _const_spec((1, D)),
        ],
        out_specs=_tok_spec(tm, D),
        scratch_shapes=[pltpu.VMEM((tm, D), F32), pltpu.VMEM((tm, D), F32), pltpu.SemaphoreType.DMA((2,))],
        compiler_params=_cparams("arbitrary", "arbitrary"),
        name="moe_combine",
    )(pos, ys, gates, X, mod, gpost.reshape(1, D))


def _moe(f, logits_t, w_in, w_out, X, mod, gpost, n_lat):
    B, n, D = f.shape
    ne = logits_t.shape[1]
    tm = EXPERT_TILE
    ints, gates, counts = _route(logits_t)
    counts = counts[:, 0].astype(jnp.int32)
    padded = ((counts + tm - 1) // tm) * tm
    ends = jnp.cumsum(padded)
    starts = ends - padded
    n_tiles = (2 * B * n) // tm + ne
    pos = jnp.stack([starts[ints[:, 0, :]] + ints[:, 2, :], starts[ints[:, 1, :]] + ints[:, 3, :]], axis=1)
    tile_start = jnp.arange(n_tiles, dtype=jnp.int32) * tm
    tile_active = (tile_start < ends[-1]).astype(jnp.int32)
    tile_expert = jnp.sum((tile_start[:, None] >= ends[None, :]).astype(jnp.int32), axis=1)
    last_expert = jnp.sum((ends[-1] - 1 >= ends).astype(jnp.int32))
    tile_expert = jnp.where(tile_active == 1, tile_expert, last_expert).astype(jnp.int32)
    xs = _dispatch(pos, f, n_tiles * tm)
    ys = _experts(tile_expert, tile_active, xs, w_in, w_out)
    g2 = jnp.transpose(gates[:, 0:2, :], (0, 2, 1))
    return _combine(pos, ys, g2, X, mod, gpost, n_lat, n)


def _rope_tables(n_lat, n_ctx, rot_dim):
    rows = n_lat // GRID_W
    row_idx = jnp.repeat(jnp.arange(rows, dtype=F32), GRID_W)
    col_idx = jnp.tile(jnp.arange(GRID_W, dtype=F32), rows)
    n_freq = rot_dim // 4
    inv_freq = ROPE_THETA ** (-jnp.arange(n_freq, dtype=F32) / n_freq)
    ang = jnp.stack([row_idx[:, None] * inv_freq, col_idx[:, None] * inv_freq], axis=1)
    ang = jnp.broadcast_to(ang[:, :, None, :], (n_lat, 2, 2, n_freq)).reshape(n_lat, rot_dim)
    cos = jnp.cos(ang)
    sin = jnp.sin(ang)
    first = (jnp.arange(rot_dim) % (2 * n_freq)) < n_freq
    sin_lo = jnp.where(first[None, :], -sin, 0.0)
    sin_hi = jnp.where(first[None, :], 0.0, sin)

    def finish(t, ctx_value):
        t = jnp.concatenate([t, jnp.full((n_ctx, rot_dim), ctx_value, F32)], axis=0)
        return jnp.pad(t, ((0, 0), (0, LANES - rot_dim)))

    return finish(cos, 1.0), finish(sin_lo, 0.0), finish(sin_hi, 0.0)


def kernel(x, c, ctx, c_ctx, ada_w, ada_b, g_pre_mix, g_post_mix, g_pre_ffn, g_post_ffn, ab_w_in, a_norm_g, a_ws,
           a_bs, b_qnorm_g, b_knorm_g, ab_w_out, ffn_w_in, ffn_w_out, mla_w_dq, mla_qnorm_g, mla_w_uq, mla_w_dkv,
           mla_kvnorm_g, mla_w_ukv, mla_w_o, moe_router, moe_w_in, moe_w_out):
    B, S, D = x.shape
    NC = ctx.shape[1]
    depth = ada_w.shape[0]
    assert S % TOKEN_TILE == 0 and S % GRID_W == 0 and NC % ROW_DMA_TILE == 0 and TOKEN_TILE % NC == 0

    X = jnp.concatenate([x, ctx], axis=1)
    cpad = jnp.zeros((8, D), F32).at[:B].set(c).at[B].set(c_ctx)
    mods = _adaln(cpad, ada_w, ada_b)
    tables_b = _rope_tables(S, NC, B_HEAD_DIM)
    tables_c = _rope_tables(S, NC, C_ROPE)

    for layer in range(depth):
        need_ctx = layer < depth - 1
        i = layer // 2
        mod = mods[layer]
        n_rows = S + NC if need_ctx else S
        if layer % 2 == 0:
            u, vg, q, k, v = _ab_in(X, mod, g_pre_mix[layer], ab_w_in[i].astype(BF16), a_norm_g[i], b_qnorm_g[i],
                                    b_knorm_g[i], tables_b, S)
            o = _attention(q, k, v, n_lat=S, n_ctx=NC, need_ctx=True, name="gqa", n_kv_heads=B_KV_HEADS,
                           group=B_HEADS // B_KV_HEADS, dk=B_HEAD_DIM, dv=B_HEAD_DIM, tq=256)
            bias = jnp.broadcast_to(a_bs[i].T[:, :, None], (CHUNK, A_GROUPS, A_GROUP_DIM)).reshape(CHUNK, A_WIDTH)
            X, f = _ab_out(u, vg, o, a_ws[i].astype(BF16), bias, ab_w_out[i].astype(BF16), X, mod,
                           g_post_mix[layer], g_pre_ffn[layer], S)
            dff = ffn_w_out.shape[1]
            nch = dff // FFN_CHUNK
            w_in = ffn_w_in[i].astype(BF16)
            wg = jnp.transpose(w_in[:, :dff].reshape(D, nch, FFN_CHUNK), (1, 0, 2))
            wu = jnp.transpose(w_in[:, dff:].reshape(D, nch, FFN_CHUNK), (1, 0, 2))
            wo = ffn_w_out[i].astype(BF16).reshape(nch, FFN_CHUNK, D)
            X = _ffn(f, wg, wu, wo, X, mod, g_post_ffn[layer], S)
        else:
            hw = 2 * LANES
            wuq = mla_w_uq[i].reshape(-1, C_HEADS, C_NOPE + C_ROPE)
            wuq = jnp.pad(wuq, ((0, 0), (0, 0), (0, hw - C_NOPE - C_ROPE))).reshape(-1, C_HEADS * hw)
            wdkv = jnp.pad(mla_w_dkv[i], ((0, 0), (0, LANES - C_ROPE)))
            wukv = mla_w_ukv[i].reshape(C_KV_RANK, C_HEADS, C_NOPE + C_V)
            wuk = wukv[:, :, :C_NOPE].reshape(C_KV_RANK, C_HEADS * C_NOPE)
            wuv = wukv[:, :, C_NOPE:].reshape(C_KV_RANK, C_HEADS * C_V)
            q, k, v = _mla_in(X, mod, g_pre_mix[layer], mla_w_dq[i].astype(BF16), mla_qnorm_g[i],
                              wuq.astype(BF16), wdkv.astype(BF16), mla_kvnorm_g[i], wuk.astype(BF16),
                              wuv.astype(BF16), tables_c, S)
            o = _attention(q, k, v, n_lat=S, n_ctx=NC, need_ctx=need_ctx, name="mla", n_kv_heads=C_HEADS,
                           group=1, dk=hw, dv=C_V, tq=512)
            X, f, logits_t = _mla_out(o, mla_w_o[i].astype(BF16), X, mod, g_post_mix[layer], g_pre_ffn[layer],
                                      moe_router[i].T, S, n_rows)
            X = _moe(f, logits_t, moe_w_in[i].astype(BF16), moe_w_out[i].astype(BF16), X, mod,
                     g_post_ffn[layer], S)
    return X[:, :S] if X.shape[1] != S else X
```

```python
import functools
import math

import jax
import jax.numpy as jnp
from jax import lax
from jax.experimental import pallas as pl
from jax.experimental.pallas import tpu as pltpu

F32 = jnp.float32
BF16 = jnp.bfloat16

GRID_W = 64
ROPE_THETA = 10000.0
EPS = 1e-6
CHUNK = 128
A_GROUPS = 4
A_GROUP_DIM = 128
A_WIDTH = A_GROUPS * A_GROUP_DIM
B_HEADS = 4
B_KV_HEADS = 2
B_HEAD_DIM = 128
C_HEADS = 8
C_KV_RANK = 256
C_NOPE = 128
C_ROPE = 64
C_V = 128
N_EXPERTS = 8
LANES = 128

VMEM_LIMIT_BYTES = 56 * 1024 * 1024
TOKEN_TILE = 512
ROW_DMA_TILE = 256
FLASH_KV_CHUNK = 2816
FLASH_UNROLL = 3
EXPERT_TILE = 512
EXPERT_FF_TILE = 1792
EXPERT_FF_CHUNK = 256
ROW_DMA_UNROLL = 8
FFN_CHUNK = 256
LOG2E = math.log2(math.e)


def _cparams(*sem):
    return pltpu.CompilerParams(dimension_semantics=sem, vmem_limit_bytes=VMEM_LIMIT_BYTES)


def _dot(a, b):
    return jnp.dot(a, b, preferred_element_type=F32)


def _dot_nt(a, b):
    return lax.dot_general(a, b, (((1,), (1,)), ((), ())), preferred_element_type=F32)


def _rms(x, g):
    return x * lax.rsqrt(jnp.mean(x * x, axis=-1, keepdims=True) + EPS) * g


def _gelu(x):
    c = math.sqrt(2.0 / math.pi)
    return x * (0.5 * (1.0 + jnp.tanh(c * (x + 0.044715 * (x * x * x)))))


def _silu(x):
    return x * (1.0 / (1.0 + jnp.exp(-x)))


def _rope(x, cos, sin_lo, sin_hi, quarter):
    return (x * cos + pltpu.roll(x, LANES - quarter, 1) * sin_lo
            + pltpu.roll(x, quarter, 1) * sin_hi)


def _modulated(x, g, mod, shift_row, scale_row):
    return (_rms(x, g) * (1.0 + mod[scale_row:scale_row + 1, :])
            + mod[shift_row:shift_row + 1, :])


def _adaln_kernel(c_ref, w_ref, b_ref, o_ref):
    act = _silu(c_ref[...]).astype(BF16)
    o_ref[0, 0] = _dot(act, w_ref[0].astype(BF16)) + b_ref[0, 0]


def _adaln(cpad, ada_w, ada_b):
    depth, d, _ = ada_w.shape
    rows = cpad.shape[0]
    out = pl.pallas_call(
        _adaln_kernel,
        out_shape=jax.ShapeDtypeStruct((depth, 6, rows, d), F32),
        grid=(depth, 6),
        in_specs=[
            pl.BlockSpec((rows, d), lambda l, j: (0, 0)),
            pl.BlockSpec((1, d, d), lambda l, j: (l, 0, j)),
            pl.BlockSpec((1, 1, 1, d), lambda l, j: (l, j, 0, 0)),
        ],
        out_specs=pl.BlockSpec((1, 1, rows, d), lambda l, j: (l, j, 0, 0)),
        compiler_params=_cparams("arbitrary", "arbitrary"),
        name="adaln",
    )(cpad, ada_w, ada_b.reshape(depth, 6, 1, d))
    return jnp.transpose(out, (0, 2, 1, 3))


def _tok_spec(tm, width):
    return pl.BlockSpec((1, tm, width), lambda b, t: (b, t, 0))


def _mod_spec(n_lat_tiles, ctx_row, d):
    return pl.BlockSpec((1, 6, d), lambda b, t: (jnp.where(t >= n_lat_tiles, ctx_row, b), 0, 0))


def _const_spec(shape):
    nd = len(shape)
    return pl.BlockSpec(shape, lambda b, t: (0,) * nd)


def _table_spec(tm):
    return pl.BlockSpec((tm, LANES), lambda b, t: (t, 0))


def _ab_in_kernel(x_ref, mod_ref, gpre_ref, w_ref, ang_ref, qg_ref, kg_ref, cos_ref, slo_ref, shi_ref,
                  u_ref, vg_ref, q_ref, k_ref, v_ref, *, qscale):
    hb = _modulated(x_ref[0], gpre_ref[...], mod_ref[0], 0, 1).astype(BF16)
    cos = cos_ref[...]
    slo = slo_ref[...]
    shi = shi_ref[...]
    quarter = B_HEAD_DIM // 4
    u_ref[0] = _gelu(_dot(hb, w_ref[:, 0:A_WIDTH])).astype(BF16)
    v = _gelu(_dot(hb, w_ref[:, A_WIDTH:2 * A_WIDTH]))
    for g in range(A_GROUPS):
        sl = slice(g * A_GROUP_DIM, (g + 1) * A_GROUP_DIM)
        vg_ref[0, :, sl] = _rms(v[:, sl], ang_ref[:, sl]).astype(BF16)
    c0 = 2 * A_WIDTH
    q = _dot(hb, w_ref[:, c0:c0 + B_HEADS * B_HEAD_DIM])
    for h in range(B_HEADS):
        sl = slice(h * B_HEAD_DIM, (h + 1) * B_HEAD_DIM)
        qr = _rope(_rms(q[:, sl], qg_ref[...]), cos, slo, shi, quarter)
        q_ref[0, :, sl] = (qr * qscale).astype(BF16)
    c1 = c0 + B_HEADS * B_HEAD_DIM
    k = _dot(hb, w_ref[:, c1:c1 + B_KV_HEADS * B_HEAD_DIM])
    for h in range(B_KV_HEADS):
        sl = slice(h * B_HEAD_DIM, (h + 1) * B_HEAD_DIM)
        k_ref[0, :, sl] = _rope(_rms(k[:, sl], kg_ref[...]), cos, slo, shi, quarter).astype(BF16)
    c2 = c1 + B_KV_HEADS * B_HEAD_DIM
    v_ref[0] = _dot(hb, w_ref[:, c2:c2 + B_KV_HEADS * B_HEAD_DIM]).astype(BF16)


def _ab_in(X, mod, gpre, w_in, a_norm_g, qg, kg, tables, n_lat):
    B, T, D = X.shape
    tm = TOKEN_TILE
    qw = B_HEADS * B_HEAD_DIM
    kw = B_KV_HEADS * B_HEAD_DIM
    qscale = (B_HEAD_DIM ** -0.5) * LOG2E
    outs = pl.pallas_call(
        functools.partial(_ab_in_kernel, qscale=qscale),
        out_shape=[
            jax.ShapeDtypeStruct((B, T, A_WIDTH), BF16),
            jax.ShapeDtypeStruct((B, T, A_WIDTH), BF16),
            jax.ShapeDtypeStruct((B, T, qw), BF16),
            jax.ShapeDtypeStruct((B, T, kw), BF16),
            jax.ShapeDtypeStruct((B, T, kw), BF16),
        ],
        grid=(B, pl.cdiv(T, tm)),
        in_specs=[
            _tok_spec(tm, D),
            _mod_spec(n_lat // tm, B, D),
            _const_spec((1, D)),
            _const_spec(w_in.shape),
            _const_spec((1, A_WIDTH)),
            _const_spec((1, B_HEAD_DIM)),
            _const_spec((1, B_HEAD_DIM)),
            _table_spec(tm), _table_spec(tm), _table_spec(tm),
        ],
        out_specs=[_tok_spec(tm, A_WIDTH), _tok_spec(tm, A_WIDTH), _tok_spec(tm, qw),
                   _tok_spec(tm, kw), _tok_spec(tm, kw)],
        compiler_params=_cparams("arbitrary", "arbitrary"),
        name="ab_in",
    )(X, mod, gpre.reshape(1, D), w_in, a_norm_g.reshape(1, A_WIDTH), qg.reshape(1, -1), kg.reshape(1, -1),
      *tables)
    return outs


def _flash_kernel(q_ref, k_ref, v_ref, o_ref, m_ref, l_ref, acc_ref, *, group, dk, dv, kv_len, tk):
    tq = q_ref.shape[1]
    if group == 1:
        q = q_ref[0]
    else:
        q = jnp.concatenate([q_ref[0, :, g * dk:(g + 1) * dk] for g in range(group)], axis=0)
    m_ref[...] = jnp.full(m_ref.shape, -jnp.inf, F32)
    l_ref[...] = jnp.zeros(l_ref.shape, F32)
    acc_ref[...] = jnp.zeros(acc_ref.shape, F32)

    def step(start, size):
        k = k_ref[0, pl.ds(start, size), :]
        v = v_ref[0, pl.ds(start, size), :]
        s = _dot_nt(q, k)
        m_prev = m_ref[...]
        m_new = jnp.maximum(m_prev, jnp.max(s, axis=-1, keepdims=True))
        alpha = jnp.exp2(m_prev - m_new)
        p = [jnp.exp2(s[:, t * LANES:(t + 1) * LANES] - m_new) for t in range(size // LANES)]
        l_ref[...] = alpha * l_ref[...] + functools.reduce(lambda a, b: a + b, p)
        pb = jnp.concatenate([t.astype(BF16) for t in p], axis=1)
        acc_ref[...] = alpha * acc_ref[...] + _dot(pb, v)
        m_ref[...] = m_new

    n_full = kv_len // tk
    if n_full > 0:
        def body(c, carry):
            step(pl.multiple_of(c * tk, tk), tk)
            return carry
        lax.fori_loop(0, n_full, body, 0, unroll=math.gcd(n_full, FLASH_UNROLL))
    tail = kv_len - n_full * tk
    if tail > 0:
        step(n_full * tk, tail)
    out = acc_ref[...] * (1.0 / jnp.sum(l_ref[...], axis=-1, keepdims=True))
    for g in range(group):
        o_ref[0, :, g * dv:(g + 1) * dv] = out[g * tq:(g + 1) * tq].astype(o_ref.dtype)


def _flash(q, k, v, *, n_kv_heads, group, dk, dv, tq, q_row0, n_q_rows, kv_row0, kv_len, name):
    B = q.shape[0]
    q_blk0 = q_row0 // tq
    kv_blk0 = kv_row0 // kv_len
    tk = min(FLASH_KV_CHUNK, kv_len)
    return pl.pallas_call(
        functools.partial(_flash_kernel, group=group, dk=dk, dv=dv, kv_len=kv_len, tk=tk),
        out_shape=jax.ShapeDtypeStruct((B, n_q_rows, n_kv_heads * group * dv), BF16),
        grid=(B, n_kv_heads, n_q_rows // tq),
        in_specs=[
            pl.BlockSpec((1, tq, group * dk), lambda b, h, i: (b, q_blk0 + i, h)),
            pl.BlockSpec((1, kv_len, dk), lambda b, h, i: (b, kv_blk0, h)),
            pl.BlockSpec((1, kv_len, dv), lambda b, h, i: (b, kv_blk0, h)),
        ],
        out_specs=pl.BlockSpec((1, tq, group * dv), lambda b, h, i: (b, i, h)),
        scratch_shapes=[
            pltpu.VMEM((group * tq, LANES), F32),
            pltpu.VMEM((group * tq, LANES), F32),
            pltpu.VMEM((group * tq, dv), F32),
        ],
        compiler_params=_cparams("arbitrary", "arbitrary", "arbitrary"),
        name=name,
    )(q, k, v)


def _attention(q, k, v, *, n_lat, n_ctx, need_ctx, name, **kw):
    T = n_lat + n_ctx
    o_lat = _flash(q, k, v, q_row0=0, n_q_rows=n_lat, kv_row0=0, kv_len=T, name=name + "_lat", **kw)
    if not need_ctx:
        return o_lat
    kw = dict(kw, tq=min(kw["tq"], n_ctx))
    o_ctx = _flash(q, k, v, q_row0=n_lat, n_q_rows=n_ctx, kv_row0=n_lat, kv_len=n_ctx,
                   name=name + "_ctx", **kw)
    return jnp.concatenate([o_lat, o_ctx], axis=1)


def _post_mix(y, x, mod, gpost, gpre):
    xn = x + mod[2:3, :] * _rms(y, gpost)
    return xn, _modulated(xn, gpre, mod, 3, 4)


def _ab_out_kernel(u_ref, vg_ref, o_ref, ws_ref, bias_ref, wout_ref, x_ref, mod_ref, gpost_ref, gpre_ref,
                   xo_ref, f_ref):
    tm = u_ref.shape[1]
    y = _dot(o_ref[0], wout_ref[A_WIDTH:, :])
    gated = []
    for c in range(tm // CHUNK):
        rows = slice(c * CHUNK, (c + 1) * CHUNK)
        mixed = jnp.concatenate(
            [_dot(ws_ref[g], vg_ref[0, rows, g * A_GROUP_DIM:(g + 1) * A_GROUP_DIM]) for g in range(A_GROUPS)],
            axis=1) + bias_ref[...]
        gated.append((u_ref[0, rows, :].astype(F32) * mixed).astype(BF16))
    y = y + _dot(jnp.concatenate(gated, axis=0), wout_ref[:A_WIDTH, :])
    xn, f = _post_mix(y, x_ref[0], mod_ref[0], gpost_ref[...], gpre_ref[...])
    xo_ref[0] = xn
    f_ref[0] = f.astype(f_ref.dtype)


def _ab_out(u, vg, o, ws, bias, w_out, X, mod, gpost, gpre, n_lat):
    B, T, D = X.shape
    tm = TOKEN_TILE
    return pl.pallas_call(
        _ab_out_kernel,
        out_shape=[jax.ShapeDtypeStruct((B, T, D), F32), jax.ShapeDtypeStruct((B, T, D), BF16)],
        grid=(B, pl.cdiv(T, tm)),
        in_specs=[
            _tok_spec(tm, A_WIDTH), _tok_spec(tm, A_WIDTH), _tok_spec(tm, o.shape[2]),
            _const_spec(ws.shape), _const_spec(bias.shape), _const_spec(w_out.shape),
            _tok_spec(tm, D), _mod_spec(n_lat // tm, B, D), _const_spec((1, D)), _const_spec((1, D)),
        ],
        out_specs=[_tok_spec(tm, D), _tok_spec(tm, D)],
        compiler_params=_cparams("arbitrary", "arbitrary"),
        name="ab_out",
    )(u, vg, o, ws, bias, w_out, X, mod, gpost.reshape(1, D), gpre.reshape(1, D))


def _ffn_kernel(f_ref, wg_ref, wu_ref, wo_ref, x_ref, mod_ref, gpost_ref, xo_ref, acc_ref):
    acc_ref[...] = jnp.zeros(acc_ref.shape, F32)

    def body(c, carry):
        fb = f_ref[0]
        h = (_silu(_dot(fb, wg_ref[c])) * _dot(fb, wu_ref[c])).astype(BF16)
        acc_ref[...] += _dot(h, wo_ref[c])
        return carry

    lax.fori_loop(0, wg_ref.shape[0], body, 0)
    xo_ref[0] = x_ref[0] + mod_ref[0][5:6, :] * _rms(acc_ref[...], gpost_ref[...])


def _ffn(f, wg, wu, wo, X, mod, gpost, n_lat):
    B, T, D = X.shape
    tm = TOKEN_TILE
    return pl.pallas_call(
        _ffn_kernel,
        out_shape=jax.ShapeDtypeStruct((B, T, D), F32),
        grid=(B, pl.cdiv(T, tm)),
        in_specs=[
            _tok_spec(tm, D), _const_spec(wg.shape), _const_spec(wu.shape), _const_spec(wo.shape),
            _tok_spec(tm, D), _mod_spec(n_lat // tm, B, D), _const_spec((1, D)),
        ],
        out_specs=_tok_spec(tm, D),
        scratch_shapes=[pltpu.VMEM((tm, D), F32)],
        compiler_params=_cparams("arbitrary", "arbitrary"),
        name="ffn",
    )(f, wg, wu, wo, X, mod, gpost.reshape(1, D))


def _mla_in_kernel(x_ref, mod_ref, gpre_ref, wdq_ref, qng_ref, wuq_ref, wdkv_ref, kvng_ref, wuk_ref, wuv_ref,
                   cos_ref, slo_ref, shi_ref, q_ref, k_ref, v_ref, *, qscale):
    hb = _modulated(x_ref[0], gpre_ref[...], mod_ref[0], 0, 1).astype(BF16)
    cos = cos_ref[...]
    slo = slo_ref[...]
    shi = shi_ref[...]
    quarter = C_ROPE // 4
    hw = 2 * LANES
    cq = _rms(_dot(hb, wdq_ref[...]), qng_ref[...]).astype(BF16)
    qf = _dot(cq, wuq_ref[...])
    for h in range(C_HEADS):
        q_ref[0, :, h * hw:h * hw + LANES] = (qf[:, h * hw:h * hw + LANES] * qscale).astype(BF16)
        qr = _rope(qf[:, h * hw + LANES:(h + 1) * hw], cos, slo, shi, quarter)
        q_ref[0, :, h * hw + LANES:(h + 1) * hw] = (qr * qscale).astype(BF16)
    kvin = _dot(hb, wdkv_ref[...])
    ckv = _rms(kvin[:, :C_KV_RANK], kvng_ref[...]).astype(BF16)
    krope = _rope(kvin[:, C_KV_RANK:], cos, slo, shi, quarter).astype(BF16)
    kn = _dot(ckv, wuk_ref[...])
    for h in range(C_HEADS):
        k_ref[0, :, h * hw:h * hw + LANES] = kn[:, h * C_NOPE:(h + 1) * C_NOPE].astype(BF16)
        k_ref[0, :, h * hw + LANES:(h + 1) * hw] = krope
    v_ref[0] = _dot(ckv, wuv_ref[...]).astype(BF16)


def _mla_in(X, mod, gpre, wdq, qng, wuq, wdkv, kvng, wuk, wuv, tables, n_lat):
    B, T, D = X.shape
    tm = TOKEN_TILE
    qscale = ((C_NOPE + C_ROPE) ** -0.5) * LOG2E
    hw = 2 * LANES
    return pl.pallas_call(
        functools.partial(_mla_in_kernel, qscale=qscale),
        out_shape=[
            jax.ShapeDtypeStruct((B, T, C_HEADS * hw), BF16),
            jax.ShapeDtypeStruct((B, T, C_HEADS * hw), BF16),
            jax.ShapeDtypeStruct((B, T, C_HEADS * C_V), BF16),
        ],
        grid=(B, pl.cdiv(T, tm)),
        in_specs=[
            _tok_spec(tm, D), _mod_spec(n_lat // tm, B, D), _const_spec((1, D)),
            _const_spec(wdq.shape), _const_spec((1, qng.shape[0])), _const_spec(wuq.shape),
            _const_spec(wdkv.shape), _const_spec((1, kvng.shape[0])), _const_spec(wuk.shape),
            _const_spec(wuv.shape),
            _table_spec(tm), _table_spec(tm), _table_spec(tm),
        ],
        out_specs=[_tok_spec(tm, C_HEADS * hw), _tok_spec(tm, C_HEADS * hw), _tok_spec(tm, C_HEADS * C_V)],
        compiler_params=_cparams("arbitrary", "arbitrary"),
        name="mla_in",
    )(X, mod, gpre.reshape(1, D), wdq, qng.reshape(1, -1), wuq, wdkv, kvng.reshape(1, -1), wuk, wuv, *tables)


def _split_bf16(x):
    hi = x.astype(BF16)
    lo = (x - hi.astype(F32)).astype(BF16)
    return hi, lo


def _mla_out_kernel(o_ref, wo_ref, x_ref, mod_ref, gpost_ref, gpre_ref, rt_ref, xo_ref, f_ref, lg_ref):
    y = _dot(o_ref[0], wo_ref[...])
    xn, f = _post_mix(y, x_ref[0], mod_ref[0], gpost_ref[...], gpre_ref[...])
    xo_ref[0] = xn
    f_ref[0] = f
    f_hi, f_lo = _split_bf16(f)
    r_hi, r_lo = _split_bf16(rt_ref[...])
    lg_ref[0] = _dot_nt(r_hi, f_hi) + (_dot_nt(r_hi, f_lo) + _dot_nt(r_lo, f_hi))


def _mla_out(o, w_o, X, mod, gpost, gpre, router_t, n_lat, n_rows):
    B, _, D = X.shape
    tm = TOKEN_TILE
    ne = router_t.shape[0]
    return pl.pallas_call(
        _mla_out_kernel,
        out_shape=[jax.ShapeDtypeStruct((B, n_rows, D), F32), jax.ShapeDtypeStruct((B, n_rows, D), F32),
                   jax.ShapeDtypeStruct((B, ne, n_rows), F32)],
        grid=(B, pl.cdiv(n_rows, tm)),
        in_specs=[
            _tok_spec(tm, D), _const_spec(w_o.shape), _tok_spec(tm, D), _mod_spec(n_lat // tm, B, D),
            _const_spec((1, D)), _const_spec((1, D)), _const_spec(router_t.shape),
        ],
        out_specs=[_tok_spec(tm, D), _tok_spec(tm, D), pl.BlockSpec((1, ne, tm), lambda b, t: (b, 0, t))],
        compiler_params=_cparams("arbitrary", "arbitrary"),
        name="mla_out",
    )(o, w_o, X, mod, gpost.reshape(1, D), gpre.reshape(1, D), router_t)


def _route_kernel(lg_ref, ints_ref, gates_ref, cnt_ref, carry_ref):
    first = jnp.logical_and(pl.program_id(0) == 0, pl.program_id(1) == 0)

    @pl.when(first)
    def _():
        carry_ref[...] = jnp.zeros(carry_ref.shape, F32)

    lg = lg_ref[0]
    ne, tn = lg.shape
    eidx = lax.broadcasted_iota(jnp.int32, (ne, tn), 0)
    m1 = jnp.max(lg, axis=0, keepdims=True)
    i1 = jnp.min(jnp.where(lg == m1, eidx, ne), axis=0, keepdims=True)
    rest = jnp.where(eidx == i1, -jnp.inf, lg)
    m2 = jnp.max(rest, axis=0, keepdims=True)
    i2 = jnp.min(jnp.where(rest == m2, eidx, ne), axis=0, keepdims=True)
    e2 = jnp.exp(m2 - m1)
    den = 1.0 + e2
    w1 = 1.0 / den
    w2 = e2 / den
    member = jnp.logical_or(eidx == i1, eidx == i2)
    before = lax.broadcasted_iota(jnp.int32, (tn, tn), 0) < lax.broadcasted_iota(jnp.int32, (tn, tn), 1)
    cum = _dot(member.astype(BF16), before.astype(BF16)) + carry_ref[:, 0:1]
    r1 = jnp.sum(jnp.where(eidx == i1, cum, 0.0), axis=0, keepdims=True)
    r2 = jnp.sum(jnp.where(eidx == i2, cum, 0.0), axis=0, keepdims=True)
    carry_ref[...] = carry_ref[...] + jnp.sum(member.astype(F32), axis=1, keepdims=True)
    row = lax.broadcasted_iota(jnp.int32, (8, tn), 0)
    ints = jnp.where(row == 0, i1, jnp.where(row == 1, i2, jnp.where(row == 2, r1.astype(jnp.int32),
                                                                      r2.astype(jnp.int32))))
    ints_ref[0] = ints
    gates_ref[0] = jnp.where(row == 0, w1, w2)
    cnt_ref[...] = carry_ref[...]


def _route(logits_t):
    B, ne, n = logits_t.shape
    tn = ROW_DMA_TILE
    return pl.pallas_call(
        _route_kernel,
        out_shape=[jax.ShapeDtypeStruct((B, 8, n), jnp.int32), jax.ShapeDtypeStruct((B, 8, n), F32),
                   jax.ShapeDtypeStruct((ne, LANES), F32)],
        grid=(B, n // tn),
        in_specs=[pl.BlockSpec((1, ne, tn), lambda b, t: (b, 0, t))],
        out_specs=[pl.BlockSpec((1, 8, tn), lambda b, t: (b, 0, t)),
                   pl.BlockSpec((1, 8, tn), lambda b, t: (b, 0, t)),
                   pl.BlockSpec((ne, LANES), lambda b, t: (0, 0))],
        scratch_shapes=[pltpu.VMEM((ne, LANES), F32)],
        compiler_params=_cparams("arbitrary", "arbitrary"),
        name="moe_route",
    )(logits_t)


def _row_copy(src_ref, src_row, dst_ref, dst_row, sem):
    return pltpu.make_async_copy(src_ref.at[pl.ds(src_row, 1), :], dst_ref.at[pl.ds(dst_row, 1), :], sem)


def _dispatch_kernel(pos_ref, f_ref, xs_in_ref, xs_ref, sem):
    del xs_in_ref
    tm = f_ref.shape[1]
    src = f_ref.at[0]

    def issue(r, carry):
        _row_copy(src, r, xs_ref, pos_ref[0, 0, r], sem.at[0]).start()
        _row_copy(src, r, xs_ref, pos_ref[0, 1, r], sem.at[1]).start()
        return carry

    lax.fori_loop(0, tm, issue, 0, unroll=ROW_DMA_UNROLL)
    for slot in range(2):
        pltpu.make_async_copy(src, xs_ref.at[pl.ds(0, tm), :], sem.at[slot]).wait()


def _dispatch(pos, f, n_sorted):
    B, n, D = f.shape
    tm = ROW_DMA_TILE
    zeros = jnp.zeros((n_sorted, D), f.dtype)
    return pl.pallas_call(
        _dispatch_kernel,
        out_shape=jax.ShapeDtypeStruct((n_sorted, D), f.dtype),
        grid=(B, n // tm),
        in_specs=[
            pl.BlockSpec((1, 2, tm), lambda b, t: (b, 0, t), memory_space=pltpu.SMEM),
            _tok_spec(tm, D),
            pl.BlockSpec(memory_space=pl.ANY),
        ],
        out_specs=pl.BlockSpec(memory_space=pl.ANY),
        scratch_shapes=[pltpu.SemaphoreType.DMA((2,))],
        input_output_aliases={2: 0},
        compiler_params=_cparams("arbitrary", "arbitrary"),
        name="moe_dispatch",
    )(pos, f, zeros)


def _expert_kernel(te_ref, ta_ref, xs_ref, wg_ref, wu_ref, wo_ref, y_ref, xb_ref, acc_ref):
    i = pl.program_id(0)
    j = pl.program_id(1)
    last = pl.num_programs(1) - 1
    active = ta_ref[i] == 1

    @pl.when(jnp.logical_and(active, j == 0))
    def _():
        xb_ref[...] = xs_ref[...].astype(BF16)
        acc_ref[...] = jnp.zeros(acc_ref.shape, F32)

    @pl.when(active)
    def _():
        xb = xb_ref[...]
        acc = acc_ref[...]
        for c in range(wg_ref.shape[2] // EXPERT_FF_CHUNK):
            cs = slice(c * EXPERT_FF_CHUNK, (c + 1) * EXPERT_FF_CHUNK)
            h = (_silu(_dot(xb, wg_ref[0, :, cs])) * _dot(xb, wu_ref[0, :, cs])).astype(BF16)
            acc = acc + _dot(h, wo_ref[0, cs, :])
        acc_ref[...] = acc

    @pl.when(jnp.logical_and(active, j == last))
    def _():
        y_ref[...] = acc_ref[...]

    @pl.when(jnp.logical_and(jnp.logical_not(active), j == last))
    def _():
        y_ref[...] = jnp.zeros(y_ref.shape, F32)


def _experts(tile_expert, tile_active, xs, w_in, w_out):
    n_sorted, D = xs.shape
    tm = EXPERT_TILE
    tf = EXPERT_FF_TILE
    dff = w_out.shape[1]
    nf = dff // tf

    def jj(i, j, ta):
        return jnp.where(ta[i] == 1, j, nf - 1)

    grid_spec = pltpu.PrefetchScalarGridSpec(
        num_scalar_prefetch=2,
        grid=(n_sorted // tm, nf),
        in_specs=[
            pl.BlockSpec((tm, D), lambda i, j, te, ta: (i, 0)),
            pl.BlockSpec((1, D, tf), lambda i, j, te, ta: (te[i], 0, jj(i, j, ta))),
            pl.BlockSpec((1, D, tf), lambda i, j, te, ta: (te[i], 0, nf + jj(i, j, ta))),
            pl.BlockSpec((1, tf, D), lambda i, j, te, ta: (te[i], jj(i, j, ta), 0)),
        ],
        out_specs=pl.BlockSpec((tm, D), lambda i, j, te, ta: (i, 0)),
        scratch_shapes=[pltpu.VMEM((tm, D), BF16), pltpu.VMEM((tm, D), F32)],
    )
    return pl.pallas_call(
        _expert_kernel,
        out_shape=jax.ShapeDtypeStruct((n_sorted, D), F32),
        grid_spec=grid_spec,
        compiler_params=_cparams("arbitrary", "arbitrary"),
        name="moe_experts",
    )(tile_expert, tile_active, xs, w_in, w_in, w_out)


def _combine_kernel(pos_ref, ys_ref, g_ref, x_ref, mod_ref, gpost_ref, xo_ref, buf_a, buf_b, sem):
    tm = x_ref.shape[1]

    def issue(r, carry):
        _row_copy(ys_ref, pos_ref[0, 0, r], buf_a, r, sem.at[0]).start()
        _row_copy(ys_ref, pos_ref[0, 1, r], buf_b, r, sem.at[1]).start()
        return carry

    lax.fori_loop(0, tm, issue, 0, unroll=ROW_DMA_UNROLL)
    pltpu.make_async_copy(ys_ref.at[pl.ds(0, tm), :], buf_a, sem.at[0]).wait()
    pltpu.make_async_copy(ys_ref.at[pl.ds(0, tm), :], buf_b, sem.at[1]).wait()
    g = g_ref[0]
    fo = g[:, 0:1] * buf_a[...] + g[:, 1:2] * buf_b[...]
    xo_ref[0] = x_ref[0] + mod_ref[0][5:6, :] * _rms(fo, gpost_ref[...])


def _combine(pos, ys, gates, X, mod, gpost, n_lat, n_rows):
    B, _, D = X.shape
    tm = ROW_DMA_TILE
    return pl.pallas_call(
        _combine_kernel,
        out_shape=jax.ShapeDtypeStruct((B, n_rows, D), F32),
        grid=(B, n_rows // tm),
        in_specs=[
            pl.BlockSpec((1, 2, tm), lambda b, t: (b, 0, t), memory_space=pltpu.SMEM),
            pl.BlockSpec(memory_space=pl.ANY),
            _tok_spec(tm, 2),
            _tok_spec(tm, D),
            _mod_spec(n_lat // tm, B, D),
            _const_spec((1, D)),
        ],
        out_specs=_tok_spec(tm, D),
        scratch_shapes=[pltpu.VMEM((tm, D), F32), pltpu.VMEM((tm, D), F32), pltpu.SemaphoreType.DMA((2,))],
        compiler_params=_cparams("arbitrary", "arbitrary"),
        name="moe_combine",
    )(pos, ys, gates, X, mod, gpost.reshape(1, D))


def _moe(f, logits_t, w_in, w_out, X, mod, gpost, n_lat):
    B, n, D = f.shape
    ne = logits_t.shape[1]
    tm = EXPERT_TILE
    ints, gates, counts = _route(logits_t)
    counts = counts[:, 0].astype(jnp.int32)
    padded = ((counts + tm - 1) // tm) * tm
    ends = jnp.cumsum(padded)
    starts = ends - padded
    n_tiles = (2 * B * n) // tm + ne
    def start_of(e):
        return sum(jnp.where(e == j, starts[j], 0) for j in range(ne))

    pos = jnp.stack([start_of(ints[:, 0, :]) + ints[:, 2, :], start_of(ints[:, 1, :]) + ints[:, 3, :]], axis=1)
    tile_start = jnp.arange(n_tiles, dtype=jnp.int32) * tm
    tile_active = (tile_start < ends[-1]).astype(jnp.int32)
    tile_expert = jnp.sum((tile_start[:, None] >= ends[None, :]).astype(jnp.int32), axis=1)
    last_expert = jnp.sum((ends[-1] - 1 >= ends).astype(jnp.int32))
    tile_expert = jnp.where(tile_active == 1, tile_expert, last_expert).astype(jnp.int32)
    xs = _dispatch(pos, f, n_tiles * tm)
    ys = _experts(tile_expert, tile_active, xs, w_in, w_out)
    g2 = jnp.transpose(gates[:, 0:2, :], (0, 2, 1))
    return _combine(pos, ys, g2, X, mod, gpost, n_lat, n)


def _rope_tables(n_lat, n_ctx, rot_dim):
    rows = n_lat // GRID_W
    row_idx = jnp.repeat(jnp.arange(rows, dtype=F32), GRID_W)
    col_idx = jnp.tile(jnp.arange(GRID_W, dtype=F32), rows)
    n_freq = rot_dim // 4
    inv_freq = ROPE_THETA ** (-jnp.arange(n_freq, dtype=F32) / n_freq)
    ang = jnp.stack([row_idx[:, None] * inv_freq, col_idx[:, None] * inv_freq], axis=1)
    ang = jnp.broadcast_to(ang[:, :, None, :], (n_lat, 2, 2, n_freq)).reshape(n_lat, rot_dim)
    cos = jnp.cos(ang)
    sin = jnp.sin(ang)
    first = (jnp.arange(rot_dim) % (2 * n_freq)) < n_freq
    sin_lo = jnp.where(first[None, :], -sin, 0.0)
    sin_hi = jnp.where(first[None, :], 0.0, sin)

    def finish(t, ctx_value):
        t = jnp.concatenate([t, jnp.full((n_ctx, rot_dim), ctx_value, F32)], axis=0)
        return jnp.pad(t, ((0, 0), (0, LANES - rot_dim)))

    return finish(cos, 1.0), finish(sin_lo, 0.0), finish(sin_hi, 0.0)


def kernel(x, c, ctx, c_ctx, ada_w, ada_b, g_pre_mix, g_post_mix, g_pre_ffn, g_post_ffn, ab_w_in, a_norm_g, a_ws,
           a_bs, b_qnorm_g, b_knorm_g, ab_w_out, ffn_w_in, ffn_w_out, mla_w_dq, mla_qnorm_g, mla_w_uq, mla_w_dkv,
           mla_kvnorm_g, mla_w_ukv, mla_w_o, moe_router, moe_w_in, moe_w_out):
    B, S, D = x.shape
    NC = ctx.shape[1]
    depth = ada_w.shape[0]
    assert S % TOKEN_TILE == 0 and S % GRID_W == 0 and NC % ROW_DMA_TILE == 0 and TOKEN_TILE % NC == 0

    X = jnp.concatenate([x, ctx], axis=1)
    cpad = jnp.zeros((8, D), F32).at[:B].set(c).at[B].set(c_ctx)
    mods = _adaln(cpad, ada_w, ada_b)
    tables_b = _rope_tables(S, NC, B_HEAD_DIM)
    tables_c = _rope_tables(S, NC, C_ROPE)

    for layer in range(depth):
        need_ctx = layer < depth - 1
        i = layer // 2
        mod = mods[layer]
        n_rows = S + NC if need_ctx else S
        if layer % 2 == 0:
            u, vg, q, k, v = _ab_in(X, mod, g_pre_mix[layer], ab_w_in[i].astype(BF16), a_norm_g[i], b_qnorm_g[i],
                                    b_knorm_g[i], tables_b, S)
            o = _attention(q, k, v, n_lat=S, n_ctx=NC, need_ctx=True, name="gqa", n_kv_heads=B_KV_HEADS,
                           group=B_HEADS // B_KV_HEADS, dk=B_HEAD_DIM, dv=B_HEAD_DIM, tq=256)
            bias = jnp.broadcast_to(a_bs[i].T[:, :, None], (CHUNK, A_GROUPS, A_GROUP_DIM)).reshape(CHUNK, A_WIDTH)
            X, f = _ab_out(u, vg, o, a_ws[i].astype(BF16), bias, ab_w_out[i].astype(BF16), X, mod,
                           g_post_mix[layer], g_pre_ffn[layer], S)
            dff = ffn_w_out.shape[1]
            nch = dff // FFN_CHUNK
            w_in = ffn_w_in[i].astype(BF16)
            wg = jnp.transpose(w_in[:, :dff].reshape(D, nch, FFN_CHUNK), (1, 0, 2))
            wu = jnp.transpose(w_in[:, dff:].reshape(D, nch, FFN_CHUNK), (1, 0, 2))
            wo = ffn_w_out[i].astype(BF16).reshape(nch, FFN_CHUNK, D)
            X = _ffn(f, wg, wu, wo, X, mod, g_post_ffn[layer], S)
        else:
            hw = 2 * LANES
            wuq = mla_w_uq[i].reshape(-1, C_HEADS, C_NOPE + C_ROPE)
            wuq = jnp.pad(wuq, ((0, 0), (0, 0), (0, hw - C_NOPE - C_ROPE))).reshape(-1, C_HEADS * hw)
            wdkv = jnp.pad(mla_w_dkv[i], ((0, 0), (0, LANES - C_ROPE)))
            wukv = mla_w_ukv[i].reshape(C_KV_RANK, C_HEADS, C_NOPE + C_V)
            wuk = wukv[:, :, :C_NOPE].reshape(C_KV_RANK, C_HEADS * C_NOPE)
            wuv = wukv[:, :, C_NOPE:].reshape(C_KV_RANK, C_HEADS * C_V)
            q, k, v = _mla_in(X, mod, g_pre_mix[layer], mla_w_dq[i].astype(BF16), mla_qnorm_g[i],
                              wuq.astype(BF16), wdkv.astype(BF16), mla_kvnorm_g[i], wuk.astype(BF16),
                              wuv.astype(BF16), tables_c, S)
            o = _attention(q, k, v, n_lat=S, n_ctx=NC, need_ctx=need_ctx, name="mla", n_kv_heads=C_HEADS,
                           group=1, dk=hw, dv=C_V, tq=512)
            X, f, logits_t = _mla_out(o, mla_w_o[i].astype(BF16), X, mod, g_post_mix[layer], g_pre_ffn[layer],
                                      moe_router[i].T, S, n_rows)
            X = _moe(f, logits_t, moe_w_in[i].astype(BF16), moe_w_out[i].astype(BF16), X, mod,
                     g_post_ffn[layer], S)
    return X[:, :S] if X.shape[1] != S else X
```

```python
import functools
import math

import jax
import jax.numpy as jnp
from jax import lax
from jax.experimental import pallas as pl
from jax.experimental.pallas import tpu as pltpu

F32 = jnp.float32
BF16 = jnp.bfloat16

GRID_W = 64
ROPE_THETA = 10000.0
EPS = 1e-6
CHUNK = 128
A_GROUPS = 4
A_GROUP_DIM = 128
A_WIDTH = A_GROUPS * A_GROUP_DIM
B_HEADS = 4
B_KV_HEADS = 2
B_HEAD_DIM = 128
C_HEADS = 8
C_KV_RANK = 256
C_NOPE = 128
C_ROPE = 64
C_V = 128
N_EXPERTS = 8
LANES = 128

VMEM_LIMIT_BYTES = 56 * 1024 * 1024
TOKEN_TILE = 512
ROW_DMA_TILE = 256
FLASH_KV_CHUNK = 2816
EXPERT_TILE = 512
EXPERT_FF_TILE = 1792
EXPERT_FF_CHUNK = 256
ROW_DMA_UNROLL = 8
FFN_CHUNK = 256
LOG2E = math.log2(math.e)


def _cparams(*sem):
    return pltpu.CompilerParams(dimension_semantics=sem, vmem_limit_bytes=VMEM_LIMIT_BYTES)


def _dot(a, b):
    return jnp.dot(a, b, preferred_element_type=F32)


def _dot_nt(a, b):
    return lax.dot_general(a, b, (((1,), (1,)), ((), ())), preferred_element_type=F32)


def _rms(x, g):
    return x * lax.rsqrt(jnp.mean(x * x, axis=-1, keepdims=True) + EPS) * g


def _gelu(x):
    c = math.sqrt(2.0 / math.pi)
    return x * (0.5 * (1.0 + jnp.tanh(c * (x + 0.044715 * (x * x * x)))))


def _silu(x):
    return x * (1.0 / (1.0 + jnp.exp(-x)))


def _rope(x, cos, sin_lo, sin_hi, quarter):
    return (x * cos + pltpu.roll(x, LANES - quarter, 1) * sin_lo
            + pltpu.roll(x, quarter, 1) * sin_hi)


def _modulated(x, g, mod, shift_row, scale_row):
    return (_rms(x, g) * (1.0 + mod[scale_row:scale_row + 1, :])
            + mod[shift_row:shift_row + 1, :])


def _adaln_kernel(c_ref, w_ref, b_ref, o_ref):
    act = _silu(c_ref[...]).astype(BF16)
    o_ref[0, 0] = _dot(act, w_ref[0].astype(BF16)) + b_ref[0, 0]


def _adaln(cpad, ada_w, ada_b):
    depth, d, _ = ada_w.shape
    rows = cpad.shape[0]
    out = pl.pallas_call(
        _adaln_kernel,
        out_shape=jax.ShapeDtypeStruct((depth, 6, rows, d), F32),
        grid=(depth, 6),
        in_specs=[
            pl.BlockSpec((rows, d), lambda l, j: (0, 0)),
            pl.BlockSpec((1, d, d), lambda l, j: (l, 0, j)),
            pl.BlockSpec((1, 1, 1, d), lambda l, j: (l, j, 0, 0)),
        ],
        out_specs=pl.BlockSpec((1, 1, rows, d), lambda l, j: (l, j, 0, 0)),
        compiler_params=_cparams("arbitrary", "arbitrary"),
        name="adaln",
    )(cpad, ada_w, ada_b.reshape(depth, 6, 1, d))
    return jnp.transpose(out, (0, 2, 1, 3))


def _tok_spec(tm, width):
    return pl.BlockSpec((1, tm, width), lambda b, t: (b, t, 0))


def _mod_spec(n_lat_tiles, ctx_row, d):
    return pl.BlockSpec((1, 6, d), lambda b, t: (jnp.where(t >= n_lat_tiles, ctx_row, b), 0, 0))


def _const_spec(shape):
    nd = len(shape)
    return pl.BlockSpec(shape, lambda b, t: (0,) * nd)


def _table_spec(tm):
    return pl.BlockSpec((tm, LANES), lambda b, t: (t, 0))


def _ab_in_kernel(x_ref, mod_ref, gpre_ref, w_ref, ang_ref, qg_ref, kg_ref, cos_ref, slo_ref, shi_ref,
                  u_ref, vg_ref, q_ref, k_ref, v_ref, *, qscale):
    hb = _modulated(x_ref[0], gpre_ref[...], mod_ref[0], 0, 1).astype(BF16)
    cos = cos_ref[...]
    slo = slo_ref[...]
    shi = shi_ref[...]
    quarter = B_HEAD_DIM // 4
    u_ref[0] = _gelu(_dot(hb, w_ref[:, 0:A_WIDTH])).astype(BF16)
    v = _gelu(_dot(hb, w_ref[:, A_WIDTH:2 * A_WIDTH]))
    for g in range(A_GROUPS):
        sl = slice(g * A_GROUP_DIM, (g + 1) * A_GROUP_DIM)
        vg_ref[0, :, sl] = _rms(v[:, sl], ang_ref[:, sl]).astype(BF16)
    c0 = 2 * A_WIDTH
    q = _dot(hb, w_ref[:, c0:c0 + B_HEADS * B_HEAD_DIM])
    for h in range(B_HEADS):
        sl = slice(h * B_HEAD_DIM, (h + 1) * B_HEAD_DIM)
        qr = _rope(_rms(q[:, sl], qg_ref[...]), cos, slo, shi, quarter)
        q_ref[0, :, sl] = (qr * qscale).astype(BF16)
    c1 = c0 + B_HEADS * B_HEAD_DIM
    k = _dot(hb, w_ref[:, c1:c1 + B_KV_HEADS * B_HEAD_DIM])
    for h in range(B_KV_HEADS):
        sl = slice(h * B_HEAD_DIM, (h + 1) * B_HEAD_DIM)
        k_ref[0, :, sl] = _rope(_rms(k[:, sl], kg_ref[...]), cos, slo, shi, quarter).astype(BF16)
    c2 = c1 + B_KV_HEADS * B_HEAD_DIM
    v_ref[0] = _dot(hb, w_ref[:, c2:c2 + B_KV_HEADS * B_HEAD_DIM]).astype(BF16)


def _ab_in(X, mod, gpre, w_in, a_norm_g, qg, kg, tables, n_lat):
    B, T, D = X.shape
    tm = TOKEN_TILE
    qw = B_HEADS * B_HEAD_DIM
    kw = B_KV_HEADS * B_HEAD_DIM
    qscale = (B_HEAD_DIM ** -0.5) * LOG2E
    outs = pl.pallas_call(
        functools.partial(_ab_in_kernel, qscale=qscale),
        out_shape=[
            jax.ShapeDtypeStruct((B, T, A_WIDTH), BF16),
            jax.ShapeDtypeStruct((B, T, A_WIDTH), BF16),
            jax.ShapeDtypeStruct((B, T, qw), BF16),
            jax.ShapeDtypeStruct((B, T, kw), BF16),
            jax.ShapeDtypeStruct((B, T, kw), BF16),
        ],
        grid=(B, pl.cdiv(T, tm)),
        in_specs=[
            _tok_spec(tm, D),
            _mod_spec(n_lat // tm, B, D),
            _const_spec((1, D)),
            _const_spec(w_in.shape),
            _const_spec((1, A_WIDTH)),
            _const_spec((1, B_HEAD_DIM)),
            _const_spec((1, B_HEAD_DIM)),
            _table_spec(tm), _table_spec(tm), _table_spec(tm),
        ],
        out_specs=[_tok_spec(tm, A_WIDTH), _tok_spec(tm, A_WIDTH), _tok_spec(tm, qw),
                   _tok_spec(tm, kw), _tok_spec(tm, kw)],
        compiler_params=_cparams("arbitrary", "arbitrary"),
        name="ab_in",
    )(X, mod, gpre.reshape(1, D), w_in, a_norm_g.reshape(1, A_WIDTH), qg.reshape(1, -1), kg.reshape(1, -1),
      *tables)
    return outs


def _flash_kernel(q_ref, k_ref, v_ref, o_ref, m_ref, l_ref, acc_ref, *, group, dk, dv, kv_len, tk):
    tq = q_ref.shape[1]
    if group == 1:
        q = q_ref[0]
    else:
        q = jnp.concatenate([q_ref[0, :, g * dk:(g + 1) * dk] for g in range(group)], axis=0)
    m_ref[...] = jnp.full(m_ref.shape, -jnp.inf, F32)
    l_ref[...] = jnp.zeros(l_ref.shape, F32)
    acc_ref[...] = jnp.zeros(acc_ref.shape, F32)

    for start in range(0, kv_len, tk):
        size = min(tk, kv_len - start)
        s = _dot_nt(q, k_ref[0, start:start + size, :])
        m_prev = m_ref[...]
        m_new = jnp.maximum(m_prev, jnp.max(s, axis=-1, keepdims=True))
        alpha = jnp.exp2(m_prev - m_new)
        p = [jnp.exp2(s[:, t * LANES:(t + 1) * LANES] - m_new) for t in range(size // LANES)]
        l_ref[...] = alpha * l_ref[...] + functools.reduce(lambda a, b: a + b, p)
        pb = jnp.concatenate([t.astype(BF16) for t in p], axis=1)
        acc_ref[...] = alpha * acc_ref[...] + _dot(pb, v_ref[0, start:start + size, :])
        m_ref[...] = m_new

    out = acc_ref[...] * (1.0 / jnp.sum(l_ref[...], axis=-1, keepdims=True))
    for g in range(group):
        o_ref[0, :, g * dv:(g + 1) * dv] = out[g * tq:(g + 1) * tq].astype(o_ref.dtype)


def _flash(q, k, v, *, n_kv_heads, group, dk, dv, tq, q_row0, n_q_rows, kv_row0, kv_len, name):
    B = q.shape[0]
    q_blk0 = q_row0 // tq
    kv_blk0 = kv_row0 // kv_len
    tk = min(FLASH_KV_CHUNK, kv_len)
    return pl.pallas_call(
        functools.partial(_flash_kernel, group=group, dk=dk, dv=dv, kv_len=kv_len, tk=tk),
        out_shape=jax.ShapeDtypeStruct((B, n_q_rows, n_kv_heads * group * dv), BF16),
        grid=(B, n_kv_heads, n_q_rows // tq),
        in_specs=[
            pl.BlockSpec((1, tq, group * dk), lambda b, h, i: (b, q_blk0 + i, h)),
            pl.BlockSpec((1, kv_len, dk), lambda b, h, i: (b, kv_blk0, h)),
            pl.BlockSpec((1, kv_len, dv), lambda b, h, i: (b, kv_blk0, h)),
        ],
        out_specs=pl.BlockSpec((1, tq, group * dv), lambda b, h, i: (b, i, h)),
        scratch_shapes=[
            pltpu.VMEM((group * tq, LANES), F32),
            pltpu.VMEM((group * tq, LANES), F32),
            pltpu.VMEM((group * tq, dv), F32),
        ],
        compiler_params=_cparams("arbitrary", "arbitrary", "arbitrary"),
        name=name,
    )(q, k, v)


def _attention(q, k, v, *, n_lat, n_ctx, need_ctx, name, **kw):
    T = n_lat + n_ctx
    o_lat = _flash(q, k, v, q_row0=0, n_q_rows=n_lat, kv_row0=0, kv_len=T, name=name + "_lat", **kw)
    if not need_ctx:
        return o_lat
    kw = dict(kw, tq=min(kw["tq"], n_ctx))
    o_ctx = _flash(q, k, v, q_row0=n_lat, n_q_rows=n_ctx, kv_row0=n_lat, kv_len=n_ctx,
                   name=name + "_ctx", **kw)
    return jnp.concatenate([o_lat, o_ctx], axis=1)


def _post_mix(y, x, mod, gpost, gpre):
    xn = x + mod[2:3, :] * _rms(y, gpost)
    return xn, _modulated(xn, gpre, mod, 3, 4)


def _ab_out_kernel(u_ref, vg_ref, o_ref, ws_ref, bias_ref, wout_ref, x_ref, mod_ref, gpost_ref, gpre_ref,
                   xo_ref, f_ref):
    tm = u_ref.shape[1]
    y = _dot(o_ref[0], wout_ref[A_WIDTH:, :])
    gated = []
    for c in range(tm // CHUNK):
        rows = slice(c * CHUNK, (c + 1) * CHUNK)
        mixed = jnp.concatenate(
            [_dot(ws_ref[g], vg_ref[0, rows, g * A_GROUP_DIM:(g + 1) * A_GROUP_DIM]) for g in range(A_GROUPS)],
            axis=1) + bias_ref[...]
        gated.append((u_ref[0, rows, :].astype(F32) * mixed).astype(BF16))
    y = y + _dot(jnp.concatenate(gated, axis=0), wout_ref[:A_WIDTH, :])
    xn, f = _post_mix(y, x_ref[0], mod_ref[0], gpost_ref[...], gpre_ref[...])
    xo_ref[0] = xn
    f_ref[0] = f.astype(f_ref.dtype)


def _ab_out(u, vg, o, ws, bias, w_out, X, mod, gpost, gpre, n_lat):
    B, T, D = X.shape
    tm = TOKEN_TILE
    return pl.pallas_call(
        _ab_out_kernel,
        out_shape=[jax.ShapeDtypeStruct((B, T, D), F32), jax.ShapeDtypeStruct((B, T, D), BF16)],
        grid=(B, pl.cdiv(T, tm)),
        in_specs=[
            _tok_spec(tm, A_WIDTH), _tok_spec(tm, A_WIDTH), _tok_spec(tm, o.shape[2]),
            _const_spec(ws.shape), _const_spec(bias.shape), _const_spec(w_out.shape),
            _tok_spec(tm, D), _mod_spec(n_lat // tm, B, D), _const_spec((1, D)), _const_spec((1, D)),
        ],
        out_specs=[_tok_spec(tm, D), _tok_spec(tm, D)],
        compiler_params=_cparams("arbitrary", "arbitrary"),
        name="ab_out",
    )(u, vg, o, ws, bias, w_out, X, mod, gpost.reshape(1, D), gpre.reshape(1, D))


def _ffn_kernel(f_ref, win_ref, wo_ref, x_ref, mod_ref, gpost_ref, xo_ref):
    dff = wo_ref.shape[0]
    fb = f_ref[0]
    acc = None
    for c0 in range(0, dff, FFN_CHUNK):
        gate = _dot(fb, win_ref[:, c0:c0 + FFN_CHUNK])
        up = _dot(fb, win_ref[:, dff + c0:dff + c0 + FFN_CHUNK])
        part = _dot((_silu(gate) * up).astype(BF16), wo_ref[c0:c0 + FFN_CHUNK, :])
        acc = part if acc is None else acc + part
    xo_ref[0] = x_ref[0] + mod_ref[0][5:6, :] * _rms(acc, gpost_ref[...])


def _ffn(f, w_in, w_out, X, mod, gpost, n_lat):
    B, T, D = X.shape
    tm = TOKEN_TILE
    return pl.pallas_call(
        _ffn_kernel,
        out_shape=jax.ShapeDtypeStruct((B, T, D), F32),
        grid=(B, pl.cdiv(T, tm)),
        in_specs=[
            _tok_spec(tm, D), _const_spec(w_in.shape), _const_spec(w_out.shape),
            _tok_spec(tm, D), _mod_spec(n_lat // tm, B, D), _const_spec((1, D)),
        ],
        out_specs=_tok_spec(tm, D),
        compiler_params=_cparams("arbitrary", "arbitrary"),
        name="ffn",
    )(f, w_in, w_out, X, mod, gpost.reshape(1, D))


def _mla_in_kernel(x_ref, mod_ref, gpre_ref, wdq_ref, qng_ref, wuq_ref, wdkv_ref, kvng_ref, wuk_ref, wuv_ref,
                   cos_ref, slo_ref, shi_ref, q_ref, k_ref, v_ref, *, qscale):
    hb = _modulated(x_ref[0], gpre_ref[...], mod_ref[0], 0, 1).astype(BF16)
    cos = cos_ref[...]
    slo = slo_ref[...]
    shi = shi_ref[...]
    quarter = C_ROPE // 4
    hw = 2 * LANES
    cq = _rms(_dot(hb, wdq_ref[...]), qng_ref[...]).astype(BF16)
    qf = _dot(cq, wuq_ref[...])
    for h in range(C_HEADS):
        q_ref[0, :, h * hw:h * hw + LANES] = (qf[:, h * hw:h * hw + LANES] * qscale).astype(BF16)
        qr = _rope(qf[:, h * hw + LANES:(h + 1) * hw], cos, slo, shi, quarter)
        q_ref[0, :, h * hw + LANES:(h + 1) * hw] = (qr * qscale).astype(BF16)
    kvin = _dot(hb, wdkv_ref[...])
    ckv = _rms(kvin[:, :C_KV_RANK], kvng_ref[...]).astype(BF16)
    krope = _rope(kvin[:, C_KV_RANK:], cos, slo, shi, quarter).astype(BF16)
    kn = _dot(ckv, wuk_ref[...])
    for h in range(C_HEADS):
        k_ref[0, :, h * hw:h * hw + LANES] = kn[:, h * C_NOPE:(h + 1) * C_NOPE].astype(BF16)
        k_ref[0, :, h * hw + LANES:(h + 1) * hw] = krope
    v_ref[0] = _dot(ckv, wuv_ref[...]).astype(BF16)


def _mla_in(X, mod, gpre, wdq, qng, wuq, wdkv, kvng, wuk, wuv, tables, n_lat):
    B, T, D = X.shape
    tm = TOKEN_TILE
    qscale = ((C_NOPE + C_ROPE) ** -0.5) * LOG2E
    hw = 2 * LANES
    return pl.pallas_call(
        functools.partial(_mla_in_kernel, qscale=qscale),
        out_shape=[
            jax.ShapeDtypeStruct((B, T, C_HEADS * hw), BF16),
            jax.ShapeDtypeStruct((B, T, C_HEADS * hw), BF16),
            jax.ShapeDtypeStruct((B, T, C_HEADS * C_V), BF16),
        ],
        grid=(B, pl.cdiv(T, tm)),
        in_specs=[
            _tok_spec(tm, D), _mod_spec(n_lat // tm, B, D), _const_spec((1, D)),
            _const_spec(wdq.shape), _const_spec((1, qng.shape[0])), _const_spec(wuq.shape),
            _const_spec(wdkv.shape), _const_spec((1, kvng.shape[0])), _const_spec(wuk.shape),
            _const_spec(wuv.shape),
            _table_spec(tm), _table_spec(tm), _table_spec(tm),
        ],
        out_specs=[_tok_spec(tm, C_HEADS * hw), _tok_spec(tm, C_HEADS * hw), _tok_spec(tm, C_HEADS * C_V)],
        compiler_params=_cparams("arbitrary", "arbitrary"),
        name="mla_in",
    )(X, mod, gpre.reshape(1, D), wdq, qng.reshape(1, -1), wuq, wdkv, kvng.reshape(1, -1), wuk, wuv, *tables)


def _split_bf16(x):
    hi = x.astype(BF16)
    lo = (x - hi.astype(F32)).astype(BF16)
    return hi, lo


def _mla_out_kernel(o_ref, wo_ref, x_ref, mod_ref, gpost_ref, gpre_ref, rt_ref, xo_ref, f_ref, lg_ref):
    y = _dot(o_ref[0], wo_ref[...])
    xn, f = _post_mix(y, x_ref[0], mod_ref[0], gpost_ref[...], gpre_ref[...])
    xo_ref[0] = xn
    f_ref[0] = f
    f_hi, f_lo = _split_bf16(f)
    r_hi, r_lo = _split_bf16(rt_ref[...])
    lg_ref[0] = _dot_nt(r_hi, f_hi) + (_dot_nt(r_hi, f_lo) + _dot_nt(r_lo, f_hi))


def _mla_out(o, w_o, X, mod, gpost, gpre, router_t, n_lat, n_rows):
    B, _, D = X.shape
    tm = TOKEN_TILE
    ne = router_t.shape[0]
    return pl.pallas_call(
        _mla_out_kernel,
        out_shape=[jax.ShapeDtypeStruct((B, n_rows, D), F32), jax.ShapeDtypeStruct((B, n_rows, D), F32),
                   jax.ShapeDtypeStruct((B, ne, n_rows), F32)],
        grid=(B, pl.cdiv(n_rows, tm)),
        in_specs=[
            _tok_spec(tm, D), _const_spec(w_o.shape), _tok_spec(tm, D), _mod_spec(n_lat // tm, B, D),
            _const_spec((1, D)), _const_spec((1, D)), _const_spec(router_t.shape),
        ],
        out_specs=[_tok_spec(tm, D), _tok_spec(tm, D), pl.BlockSpec((1, ne, tm), lambda b, t: (b, 0, t))],
        compiler_params=_cparams("arbitrary", "arbitrary"),
        name="mla_out",
    )(o, w_o, X, mod, gpost.reshape(1, D), gpre.reshape(1, D), router_t)


def _route_kernel(lg_ref, ints_ref, gates_ref, cnt_ref, carry_ref):
    first = jnp.logical_and(pl.program_id(0) == 0, pl.program_id(1) == 0)

    @pl.when(first)
    def _():
        carry_ref[...] = jnp.zeros(carry_ref.shape, F32)

    lg = lg_ref[0]
    ne, tn = lg.shape
    eidx = lax.broadcasted_iota(jnp.int32, (ne, tn), 0)
    m1 = jnp.max(lg, axis=0, keepdims=True)
    i1 = jnp.min(jnp.where(lg == m1, eidx, ne), axis=0, keepdims=True)
    rest = jnp.where(eidx == i1, -jnp.inf, lg)
    m2 = jnp.max(rest, axis=0, keepdims=True)
    i2 = jnp.min(jnp.where(rest == m2, eidx, ne), axis=0, keepdims=True)
    e2 = jnp.exp(m2 - m1)
    den = 1.0 + e2
    w1 = 1.0 / den
    w2 = e2 / den
    member = jnp.logical_or(eidx == i1, eidx == i2)
    before = lax.broadcasted_iota(jnp.int32, (tn, tn), 0) < lax.broadcasted_iota(jnp.int32, (tn, tn), 1)
    cum = _dot(member.astype(BF16), before.astype(BF16)) + carry_ref[:, 0:1]
    r1 = jnp.sum(jnp.where(eidx == i1, cum, 0.0), axis=0, keepdims=True)
    r2 = jnp.sum(jnp.where(eidx == i2, cum, 0.0), axis=0, keepdims=True)
    carry_ref[...] = carry_ref[...] + jnp.sum(member.astype(F32), axis=1, keepdims=True)
    row = lax.broadcasted_iota(jnp.int32, (8, tn), 0)
    ints = jnp.where(row == 0, i1, jnp.where(row == 1, i2, jnp.where(row == 2, r1.astype(jnp.int32),
                                                                      r2.astype(jnp.int32))))
    ints_ref[0] = ints
    gates_ref[0] = jnp.where(row == 0, w1, w2)
    cnt_ref[...] = carry_ref[...]


def _route(logits_t):
    B, ne, n = logits_t.shape
    tn = ROW_DMA_TILE
    return pl.pallas_call(
        _route_kernel,
        out_shape=[jax.ShapeDtypeStruct((B, 8, n), jnp.int32), jax.ShapeDtypeStruct((B, 8, n), F32),
                   jax.ShapeDtypeStruct((ne, LANES), F32)],
        grid=(B, n // tn),
        in_specs=[pl.BlockSpec((1, ne, tn), lambda b, t: (b, 0, t))],
        out_specs=[pl.BlockSpec((1, 8, tn), lambda b, t: (b, 0, t)),
                   pl.BlockSpec((1, 8, tn), lambda b, t: (b, 0, t)),
                   pl.BlockSpec((ne, LANES), lambda b, t: (0, 0))],
        scratch_shapes=[pltpu.VMEM((ne, LANES), F32)],
        compiler_params=_cparams("arbitrary", "arbitrary"),
        name="moe_route",
    )(logits_t)


def _row_copy(src_ref, src_row, dst_ref, dst_row, sem):
    return pltpu.make_async_copy(src_ref.at[pl.ds(src_row, 1), :], dst_ref.at[pl.ds(dst_row, 1), :], sem)


def _dispatch_kernel(pos_ref, f_ref, xs_in_ref, xs_ref, sem):
    del xs_in_ref
    tm = f_ref.shape[1]
    src = f_ref.at[0]

    def issue(r, carry):
        _row_copy(src, r, xs_ref, pos_ref[0, 0, r], sem.at[0]).start()
        _row_copy(src, r, xs_ref, pos_ref[0, 1, r], sem.at[1]).start()
        return carry

    lax.fori_loop(0, tm, issue, 0, unroll=ROW_DMA_UNROLL)
    for slot in range(2):
        pltpu.make_async_copy(src, xs_ref.at[pl.ds(0, tm), :], sem.at[slot]).wait()


def _dispatch(pos, f, n_sorted):
    B, n, D = f.shape
    tm = ROW_DMA_TILE
    zeros = jnp.zeros((n_sorted, D), f.dtype)
    return pl.pallas_call(
        _dispatch_kernel,
        out_shape=jax.ShapeDtypeStruct((n_sorted, D), f.dtype),
        grid=(B, n // tm),
        in_specs=[
            pl.BlockSpec((1, 2, tm), lambda b, t: (b, 0, t), memory_space=pltpu.SMEM),
            _tok_spec(tm, D),
            pl.BlockSpec(memory_space=pl.ANY),
        ],
        out_specs=pl.BlockSpec(memory_space=pl.ANY),
        scratch_shapes=[pltpu.SemaphoreType.DMA((2,))],
        input_output_aliases={2: 0},
        compiler_params=_cparams("arbitrary", "arbitrary"),
        name="moe_dispatch",
    )(pos, f, zeros)


def _expert_kernel(te_ref, ta_ref, xs_ref, wg_ref, wu_ref, wo_ref, y_ref, xb_ref, acc_ref):
    i = pl.program_id(0)
    j = pl.program_id(1)
    last = pl.num_programs(1) - 1
    active = ta_ref[i] == 1

    @pl.when(jnp.logical_and(active, j == 0))
    def _():
        xb_ref[...] = xs_ref[...].astype(BF16)
        acc_ref[...] = jnp.zeros(acc_ref.shape, F32)

    @pl.when(active)
    def _():
        xb = xb_ref[...]
        acc = acc_ref[...]
        for c in range(wg_ref.shape[2] // EXPERT_FF_CHUNK):
            cs = slice(c * EXPERT_FF_CHUNK, (c + 1) * EXPERT_FF_CHUNK)
            h = (_silu(_dot(xb, wg_ref[0, :, cs])) * _dot(xb, wu_ref[0, :, cs])).astype(BF16)
            acc = acc + _dot(h, wo_ref[0, cs, :])
        acc_ref[...] = acc

    @pl.when(jnp.logical_and(active, j == last))
    def _():
        y_ref[...] = acc_ref[...]

    @pl.when(jnp.logical_and(jnp.logical_not(active), j == last))
    def _():
        y_ref[...] = jnp.zeros(y_ref.shape, F32)


def _experts(tile_expert, tile_active, xs, w_in, w_out, li):
    n_sorted, D = xs.shape
    tm = EXPERT_TILE
    tf = EXPERT_FF_TILE
    dff = w_out.shape[2]
    nf = dff // tf

    def jj(i, j, ta):
        return jnp.where(ta[i] == 1, j, nf - 1)

    grid_spec = pltpu.PrefetchScalarGridSpec(
        num_scalar_prefetch=2,
        grid=(n_sorted // tm, nf),
        in_specs=[
            pl.BlockSpec((tm, D), lambda i, j, te, ta: (i, 0)),
            pl.BlockSpec((None, 1, D, tf), lambda i, j, te, ta: (li, te[i], 0, jj(i, j, ta))),
            pl.BlockSpec((None, 1, D, tf), lambda i, j, te, ta: (li, te[i], 0, nf + jj(i, j, ta))),
            pl.BlockSpec((None, 1, tf, D), lambda i, j, te, ta: (li, te[i], jj(i, j, ta), 0)),
        ],
        out_specs=pl.BlockSpec((tm, D), lambda i, j, te, ta: (i, 0)),
        scratch_shapes=[pltpu.VMEM((tm, D), BF16), pltpu.VMEM((tm, D), F32)],
    )
    return pl.pallas_call(
        _expert_kernel,
        out_shape=jax.ShapeDtypeStruct((n_sorted, D), F32),
        grid_spec=grid_spec,
        compiler_params=_cparams("arbitrary", "arbitrary"),
        name="moe_experts",
    )(tile_expert, tile_active, xs, w_in, w_in, w_out)


def _combine_kernel(pos_ref, ys_ref, g_ref, x_ref, mod_ref, gpost_ref, xo_ref, buf_a, buf_b, sem):
    tm = x_ref.shape[1]

    def issue(r, carry):
        _row_copy(ys_ref, pos_ref[0, 0, r], buf_a, r, sem.at[0]).start()
        _row_copy(ys_ref, pos_ref[0, 1, r], buf_b, r, sem.at[1]).start()
        return carry

    lax.fori_loop(0, tm, issue, 0, unroll=ROW_DMA_UNROLL)
    pltpu.make_async_copy(ys_ref.at[pl.ds(0, tm), :], buf_a, sem.at[0]).wait()
    pltpu.make_async_copy(ys_ref.at[pl.ds(0, tm), :], buf_b, sem.at[1]).wait()
    g = g_ref[0]
    fo = g[:, 0:1] * buf_a[...] + g[:, 1:2] * buf_b[...]
    xo_ref[0] = x_ref[0] + mod_ref[0][5:6, :] * _rms(fo, gpost_ref[...])


def _combine(pos, ys, gates, X, mod, gpost, n_lat, n_rows):
    B, _, D = X.shape
    tm = ROW_DMA_TILE
    return pl.pallas_call(
        _combine_kernel,
        out_shape=jax.ShapeDtypeStruct((B, n_rows, D), F32),
        grid=(B, n_rows // tm),
        in_specs=[
            pl.BlockSpec((1, 2, tm), lambda b, t: (b, 0, t), memory_space=pltpu.SMEM),
            pl.BlockSpec(memory_space=pl.ANY),
            _tok_spec(tm, 2),
            _tok_spec(tm, D),
            _mod_spec(n_lat // tm, B, D),
            _const_spec((1, D)),
        ],
        out_specs=_tok_spec(tm, D),
        scratch_shapes=[pltpu.VMEM((tm, D), F32), pltpu.VMEM((tm, D), F32), pltpu.SemaphoreType.DMA((2,))],
        compiler_params=_cparams("arbitrary", "arbitrary"),
        name="moe_combine",
    )(pos, ys, gates, X, mod, gpost.reshape(1, D))


def _moe(f, logits_t, w_in, w_out, li, X, mod, gpost, n_lat):
    B, n, D = f.shape
    ne = logits_t.shape[1]
    tm = EXPERT_TILE
    ints, gates, counts = _route(logits_t)
    counts = counts[:, 0].astype(jnp.int32)
    padded = ((counts + tm - 1) // tm) * tm
    ends = jnp.cumsum(padded)
    starts = ends - padded
    n_tiles = (2 * B * n) // tm + ne
    def start_of(e):
        return sum(jnp.where(e == j, starts[j], 0) for j in range(ne))

    pos = jnp.stack([start_of(ints[:, 0, :]) + ints[:, 2, :], start_of(ints[:, 1, :]) + ints[:, 3, :]], axis=1)
    tile_start = jnp.arange(n_tiles, dtype=jnp.int32) * tm
    tile_active = (tile_start < ends[-1]).astype(jnp.int32)
    tile_expert = jnp.sum((tile_start[:, None] >= ends[None, :]).astype(jnp.int32), axis=1)
    last_expert = jnp.sum((ends[-1] - 1 >= ends).astype(jnp.int32))
    tile_expert = jnp.where(tile_active == 1, tile_expert, last_expert).astype(jnp.int32)
    xs = _dispatch(pos, f, n_tiles * tm)
    ys = _experts(tile_expert, tile_active, xs, w_in, w_out, li)
    g2 = jnp.transpose(gates[:, 0:2, :], (0, 2, 1))
    return _combine(pos, ys, g2, X, mod, gpost, n_lat, n)


def _rope_tables(n_lat, n_ctx, rot_dim):
    rows = n_lat // GRID_W
    row_idx = jnp.repeat(jnp.arange(rows, dtype=F32), GRID_W)
    col_idx = jnp.tile(jnp.arange(GRID_W, dtype=F32), rows)
    n_freq = rot_dim // 4
    inv_freq = ROPE_THETA ** (-jnp.arange(n_freq, dtype=F32) / n_freq)
    ang = jnp.stack([row_idx[:, None] * inv_freq, col_idx[:, None] * inv_freq], axis=1)
    ang = jnp.broadcast_to(ang[:, :, None, :], (n_lat, 2, 2, n_freq)).reshape(n_lat, rot_dim)
    cos = jnp.cos(ang)
    sin = jnp.sin(ang)
    first = (jnp.arange(rot_dim) % (2 * n_freq)) < n_freq
    sin_lo = jnp.where(first[None, :], -sin, 0.0)
    sin_hi = jnp.where(first[None, :], 0.0, sin)

    def finish(t, ctx_value):
        t = jnp.concatenate([t, jnp.full((n_ctx, rot_dim), ctx_value, F32)], axis=0)
        return jnp.pad(t, ((0, 0), (0, LANES - rot_dim)))

    return finish(cos, 1.0), finish(sin_lo, 0.0), finish(sin_hi, 0.0)


def kernel(x, c, ctx, c_ctx, ada_w, ada_b, g_pre_mix, g_post_mix, g_pre_ffn, g_post_ffn, ab_w_in, a_norm_g, a_ws,
           a_bs, b_qnorm_g, b_knorm_g, ab_w_out, ffn_w_in, ffn_w_out, mla_w_dq, mla_qnorm_g, mla_w_uq, mla_w_dkv,
           mla_kvnorm_g, mla_w_ukv, mla_w_o, moe_router, moe_w_in, moe_w_out):
    B, S, D = x.shape
    NC = ctx.shape[1]
    depth = ada_w.shape[0]
    assert S % TOKEN_TILE == 0 and S % GRID_W == 0 and NC % ROW_DMA_TILE == 0 and TOKEN_TILE % NC == 0

    X = jnp.concatenate([x, ctx], axis=1)
    cpad = jnp.zeros((8, D), F32).at[:B].set(c).at[B].set(c_ctx)
    mods = _adaln(cpad, ada_w, ada_b)
    tables_b = _rope_tables(S, NC, B_HEAD_DIM)
    tables_c = _rope_tables(S, NC, C_ROPE)
    moe_w_in_b = moe_w_in.astype(BF16)
    moe_w_out_b = moe_w_out.astype(BF16)

    for layer in range(depth):
        need_ctx = layer < depth - 1
        i = layer // 2
        mod = mods[layer]
        n_rows = S + NC if need_ctx else S
        if layer % 2 == 0:
            u, vg, q, k, v = _ab_in(X, mod, g_pre_mix[layer], ab_w_in[i].astype(BF16), a_norm_g[i], b_qnorm_g[i],
                                    b_knorm_g[i], tables_b, S)
            o = _attention(q, k, v, n_lat=S, n_ctx=NC, need_ctx=True, name="gqa", n_kv_heads=B_KV_HEADS,
                           group=B_HEADS // B_KV_HEADS, dk=B_HEAD_DIM, dv=B_HEAD_DIM, tq=512)
            bias = jnp.broadcast_to(a_bs[i].T[:, :, None], (CHUNK, A_GROUPS, A_GROUP_DIM)).reshape(CHUNK, A_WIDTH)
            X, f = _ab_out(u, vg, o, a_ws[i].astype(BF16), bias, ab_w_out[i].astype(BF16), X, mod,
                           g_post_mix[layer], g_pre_ffn[layer], S)
            X = _ffn(f, ffn_w_in[i].astype(BF16), ffn_w_out[i].astype(BF16), X, mod, g_post_ffn[layer], S)
        else:
            hw = 2 * LANES
            wuq = mla_w_uq[i].reshape(-1, C_HEADS, C_NOPE + C_ROPE)
            wuq = jnp.pad(wuq, ((0, 0), (0, 0), (0, hw - C_NOPE - C_ROPE))).reshape(-1, C_HEADS * hw)
            wdkv = jnp.pad(mla_w_dkv[i], ((0, 0), (0, LANES - C_ROPE)))
            wukv = mla_w_ukv[i].reshape(C_KV_RANK, C_HEADS, C_NOPE + C_V)
            wuk = wukv[:, :, :C_NOPE].reshape(C_KV_RANK, C_HEADS * C_NOPE)
            wuv = wukv[:, :, C_NOPE:].reshape(C_KV_RANK, C_HEADS * C_V)
            q, k, v = _mla_in(X, mod, g_pre_mix[layer], mla_w_dq[i].astype(BF16), mla_qnorm_g[i],
                              wuq.astype(BF16), wdkv.astype(BF16), mla_kvnorm_g[i], wuk.astype(BF16),
                              wuv.astype(BF16), tables_c, S)
            o = _attention(q, k, v, n_lat=S, n_ctx=NC, need_ctx=need_ctx, name="mla", n_kv_heads=C_HEADS,
                           group=1, dk=hw, dv=C_V, tq=1024)
            X, f, logits_t = _mla_out(o, mla_w_o[i].astype(BF16), X, mod, g_post_mix[layer], g_pre_ffn[layer],
                                      moe_router[i].T, S, n_rows)
            X = _moe(f, logits_t, moe_w_in_b, moe_w_out_b, i, X, mod, g_post_ffn[layer], S)
    return X[:, :S] if X.shape[1] != S else X
```

```python
import functools
import math

import jax
import jax.numpy as jnp
from jax import lax
from jax.experimental import pallas as pl
from jax.experimental.pallas import tpu as pltpu

F32 = jnp.float32
BF16 = jnp.bfloat16

GRID_W = 64
ROPE_THETA = 10000.0
EPS = 1e-6
CHUNK = 128
A_GROUPS = 4
A_GROUP_DIM = 128
A_WIDTH = A_GROUPS * A_GROUP_DIM
B_HEADS = 4
B_KV_HEADS = 2
B_HEAD_DIM = 128
C_HEADS = 8
C_KV_RANK = 256
C_NOPE = 128
C_ROPE = 64
C_V = 128
N_EXPERTS = 8
LANES = 128

VMEM_LIMIT_BYTES = 56 * 1024 * 1024
TOKEN_TILE = 512
TOKEN_SUBTILE = 256
ROW_DMA_TILE = 256
FLASH_KV_CHUNK = 2816
EXPERT_TILE = 512
EXPERT_FF_TILE = 1792
EXPERT_FF_CHUNK = 256
ROW_DMA_UNROLL = 8
FFN_CHUNK = 256
LOG2E = math.log2(math.e)


def _cparams(*sem):
    return pltpu.CompilerParams(dimension_semantics=sem, vmem_limit_bytes=VMEM_LIMIT_BYTES)


def _dot(a, b):
    return jnp.dot(a, b, preferred_element_type=F32)


def _dot_nt(a, b):
    return lax.dot_general(a, b, (((1,), (1,)), ((), ())), preferred_element_type=F32)


def _rms(x, g):
    return x * lax.rsqrt(jnp.mean(x * x, axis=-1, keepdims=True) + EPS) * g


def _gelu(x):
    c = math.sqrt(2.0 / math.pi)
    return x * (0.5 * (1.0 + jnp.tanh(c * (x + 0.044715 * (x * x * x)))))


def _silu(x):
    return x * (1.0 / (1.0 + jnp.exp(-x)))


def _rope(x, cos, sin_lo, sin_hi, quarter):
    return (x * cos + pltpu.roll(x, LANES - quarter, 1) * sin_lo
            + pltpu.roll(x, quarter, 1) * sin_hi)


def _modulated(x, g, mod, shift_row, scale_row):
    return (_rms(x, g) * (1.0 + mod[scale_row:scale_row + 1, :])
            + mod[shift_row:shift_row + 1, :])


def _adaln_kernel(c_ref, w_ref, b_ref, o_ref):
    act = _silu(c_ref[...]).astype(BF16)
    o_ref[0, 0] = _dot(act, w_ref[0].astype(BF16)) + b_ref[0, 0]


def _adaln(cpad, ada_w, ada_b):
    depth, d, _ = ada_w.shape
    rows = cpad.shape[0]
    out = pl.pallas_call(
        _adaln_kernel,
        out_shape=jax.ShapeDtypeStruct((depth, 6, rows, d), F32),
        grid=(depth, 6),
        in_specs=[
            pl.BlockSpec((rows, d), lambda l, j: (0, 0)),
            pl.BlockSpec((1, d, d), lambda l, j: (l, 0, j)),
            pl.BlockSpec((1, 1, 1, d), lambda l, j: (l, j, 0, 0)),
        ],
        out_specs=pl.BlockSpec((1, 1, rows, d), lambda l, j: (l, j, 0, 0)),
        compiler_params=_cparams("arbitrary", "arbitrary"),
        name="adaln",
    )(cpad, ada_w, ada_b.reshape(depth, 6, 1, d))
    return jnp.transpose(out, (0, 2, 1, 3))


def _tok_spec(tm, width):
    return pl.BlockSpec((1, tm, width), lambda b, t: (b, t, 0))


def _mod_spec(n_lat_tiles, ctx_row, d):
    return pl.BlockSpec((1, 6, d), lambda b, t: (jnp.where(t >= n_lat_tiles, ctx_row, b), 0, 0))


def _const_spec(shape):
    nd = len(shape)
    return pl.BlockSpec(shape, lambda b, t: (0,) * nd)


def _table_spec(tm):
    return pl.BlockSpec((tm, LANES), lambda b, t: (t, 0))


def _ab_in_kernel(x_ref, mod_ref, gpre_ref, w_ref, ang_ref, qg_ref, kg_ref, cos_ref, slo_ref, shi_ref,
                  u_ref, vg_ref, q_ref, k_ref, v_ref, *, qscale):
    quarter = B_HEAD_DIM // 4
    c0 = 2 * A_WIDTH
    c1 = c0 + B_HEADS * B_HEAD_DIM
    c2 = c1 + B_KV_HEADS * B_HEAD_DIM
    tm = x_ref.shape[1]
    for r0 in range(0, tm, TOKEN_SUBTILE):
        rows = slice(r0, r0 + TOKEN_SUBTILE)
        hb = _modulated(x_ref[0, rows, :], gpre_ref[...], mod_ref[0], 0, 1).astype(BF16)
        cos = cos_ref[rows, :]
        slo = slo_ref[rows, :]
        shi = shi_ref[rows, :]
        u_ref[0, rows, :] = _gelu(_dot(hb, w_ref[:, 0:A_WIDTH])).astype(BF16)
        v = _gelu(_dot(hb, w_ref[:, A_WIDTH:c0]))
        for g in range(A_GROUPS):
            sl = slice(g * A_GROUP_DIM, (g + 1) * A_GROUP_DIM)
            vg_ref[0, rows, sl] = _rms(v[:, sl], ang_ref[:, sl]).astype(BF16)
        q = _dot(hb, w_ref[:, c0:c1])
        for h in range(B_HEADS):
            sl = slice(h * B_HEAD_DIM, (h + 1) * B_HEAD_DIM)
            qr = _rope(_rms(q[:, sl], qg_ref[...]), cos, slo, shi, quarter)
            q_ref[0, rows, sl] = (qr * qscale).astype(BF16)
        k = _dot(hb, w_ref[:, c1:c2])
        for h in range(B_KV_HEADS):
            sl = slice(h * B_HEAD_DIM, (h + 1) * B_HEAD_DIM)
            k_ref[0, rows, sl] = _rope(_rms(k[:, sl], kg_ref[...]), cos, slo, shi, quarter).astype(BF16)
        v_ref[0, rows, :] = _dot(hb, w_ref[:, c2:c2 + B_KV_HEADS * B_HEAD_DIM]).astype(BF16)


def _ab_in(X, mod, gpre, w_in, a_norm_g, qg, kg, tables, n_lat):
    B, T, D = X.shape
    tm = TOKEN_TILE
    qw = B_HEADS * B_HEAD_DIM
    kw = B_KV_HEADS * B_HEAD_DIM
    qscale = (B_HEAD_DIM ** -0.5) * LOG2E
    outs = pl.pallas_call(
        functools.partial(_ab_in_kernel, qscale=qscale),
        out_shape=[
            jax.ShapeDtypeStruct((B, T, A_WIDTH), BF16),
            jax.ShapeDtypeStruct((B, T, A_WIDTH), BF16),
            jax.ShapeDtypeStruct((B, T, qw), BF16),
            jax.ShapeDtypeStruct((B, T, kw), BF16),
            jax.ShapeDtypeStruct((B, T, kw), BF16),
        ],
        grid=(B, pl.cdiv(T, tm)),
        in_specs=[
            _tok_spec(tm, D),
            _mod_spec(n_lat // tm, B, D),
            _const_spec((1, D)),
            _const_spec(w_in.shape),
            _const_spec((1, A_WIDTH)),
            _const_spec((1, B_HEAD_DIM)),
            _const_spec((1, B_HEAD_DIM)),
            _table_spec(tm), _table_spec(tm), _table_spec(tm),
        ],
        out_specs=[_tok_spec(tm, A_WIDTH), _tok_spec(tm, A_WIDTH), _tok_spec(tm, qw),
                   _tok_spec(tm, kw), _tok_spec(tm, kw)],
        compiler_params=_cparams("arbitrary", "arbitrary"),
        name="ab_in",
    )(X, mod, gpre.reshape(1, D), w_in, a_norm_g.reshape(1, A_WIDTH), qg.reshape(1, -1), kg.reshape(1, -1),
      *tables)
    return outs


def _flash_kernel(*refs, group, dk, dv, kv_len, tk, with_cast):
    if with_cast:
        q_ref, k_ref, v_ref, w_ref, o_ref, wb_ref, m_ref, l_ref, acc_ref = refs
        wb_ref[...] = w_ref[...].astype(BF16)
    else:
        q_ref, k_ref, v_ref, o_ref, m_ref, l_ref, acc_ref = refs
    tq = q_ref.shape[1]
    if group == 1:
        q = q_ref[0]
    else:
        q = jnp.concatenate([q_ref[0, :, g * dk:(g + 1) * dk] for g in range(group)], axis=0)
    m_ref[...] = jnp.full(m_ref.shape, -jnp.inf, F32)
    l_ref[...] = jnp.zeros(l_ref.shape, F32)
    acc_ref[...] = jnp.zeros(acc_ref.shape, F32)

    for start in range(0, kv_len, tk):
        size = min(tk, kv_len - start)
        s = _dot_nt(q, k_ref[0, start:start + size, :])
        m_prev = m_ref[...]
        m_new = jnp.maximum(m_prev, jnp.max(s, axis=-1, keepdims=True))
        alpha = jnp.exp2(m_prev - m_new)
        p = [jnp.exp2(s[:, t * LANES:(t + 1) * LANES] - m_new) for t in range(size // LANES)]
        l_ref[...] = alpha * l_ref[...] + functools.reduce(lambda a, b: a + b, p)
        pb = jnp.concatenate([t.astype(BF16) for t in p], axis=1)
        acc_ref[...] = alpha * acc_ref[...] + _dot(pb, v_ref[0, start:start + size, :])
        m_ref[...] = m_new

    out = acc_ref[...] * (1.0 / jnp.sum(l_ref[...], axis=-1, keepdims=True))
    for g in range(group):
        o_ref[0, :, g * dv:(g + 1) * dv] = out[g * tq:(g + 1) * tq].astype(o_ref.dtype)


def _flash(q, k, v, *, n_kv_heads, group, dk, dv, tq, q_row0, n_q_rows, kv_row0, kv_len, name, cast_w=None):
    B = q.shape[0]
    q_blk0 = q_row0 // tq
    kv_blk0 = kv_row0 // kv_len
    tk = min(FLASH_KV_CHUNK, kv_len)
    n_q = n_q_rows // tq
    n_steps = B * n_kv_heads * n_q
    in_specs = [
        pl.BlockSpec((1, tq, group * dk), lambda b, h, i: (b, q_blk0 + i, h)),
        pl.BlockSpec((1, kv_len, dk), lambda b, h, i: (b, kv_blk0, h)),
        pl.BlockSpec((1, kv_len, dv), lambda b, h, i: (b, kv_blk0, h)),
    ]
    out_specs = [pl.BlockSpec((1, tq, group * dv), lambda b, h, i: (b, i, h))]
    out_shape = [jax.ShapeDtypeStruct((B, n_q_rows, n_kv_heads * group * dv), BF16)]
    args = [q, k, v]
    if cast_w is not None:
        w_all, li = cast_w
        slabs = w_all.reshape(w_all.shape[0] * n_steps, -1, w_all.shape[-1])
        blk = (1,) + slabs.shape[1:]
        in_specs.append(pl.BlockSpec(blk, lambda b, h, i: (li * n_steps + (b * n_kv_heads + h) * n_q + i, 0, 0)))
        out_specs.append(pl.BlockSpec(blk, lambda b, h, i: ((b * n_kv_heads + h) * n_q + i, 0, 0)))
        out_shape.append(jax.ShapeDtypeStruct((n_steps,) + slabs.shape[1:], BF16))
        args.append(slabs)
    outs = pl.pallas_call(
        functools.partial(_flash_kernel, group=group, dk=dk, dv=dv, kv_len=kv_len, tk=tk,
                          with_cast=cast_w is not None),
        out_shape=out_shape,
        grid=(B, n_kv_heads, n_q),
        in_specs=in_specs,
        out_specs=out_specs,
        scratch_shapes=[
            pltpu.VMEM((group * tq, LANES), F32),
            pltpu.VMEM((group * tq, LANES), F32),
            pltpu.VMEM((group * tq, dv), F32),
        ],
        compiler_params=_cparams("arbitrary", "arbitrary", "arbitrary"),
        name=name,
    )(*args)
    if cast_w is None:
        return outs[0], None
    return outs[0], outs[1].reshape(cast_w[0].shape[1:])


def _attention(q, k, v, *, n_lat, n_ctx, need_ctx, name, cast_w=None, **kw):
    T = n_lat + n_ctx
    o, w_b = _flash(q, k, v, q_row0=0, n_q_rows=n_lat, kv_row0=0, kv_len=T, name=name + "_lat", cast_w=cast_w,
                    **kw)
    if need_ctx:
        kw = dict(kw, tq=min(kw["tq"], n_ctx))
        o_ctx, _ = _flash(q, k, v, q_row0=n_lat, n_q_rows=n_ctx, kv_row0=n_lat, kv_len=n_ctx,
                          name=name + "_ctx", **kw)
        o = jnp.concatenate([o, o_ctx], axis=1)
    return o, w_b


def _post_mix(y, x, mod, gpost, gpre):
    xn = x + mod[2:3, :] * _rms(y, gpost)
    return xn, _modulated(xn, gpre, mod, 3, 4)


def _ab_out_kernel(u_ref, vg_ref, o_ref, ws_ref, bias_ref, wout_ref, x_ref, mod_ref, gpost_ref, gpre_ref,
                   xo_ref, f_ref):
    tm = u_ref.shape[1]
    y = _dot(o_ref[0], wout_ref[A_WIDTH:, :])
    gated = []
    for c in range(tm // CHUNK):
        rows = slice(c * CHUNK, (c + 1) * CHUNK)
        mixed = jnp.concatenate(
            [_dot(ws_ref[g], vg_ref[0, rows, g * A_GROUP_DIM:(g + 1) * A_GROUP_DIM]) for g in range(A_GROUPS)],
            axis=1) + bias_ref[...]
        gated.append((u_ref[0, rows, :].astype(F32) * mixed).astype(BF16))
    y = y + _dot(jnp.concatenate(gated, axis=0), wout_ref[:A_WIDTH, :])
    xn, f = _post_mix(y, x_ref[0], mod_ref[0], gpost_ref[...], gpre_ref[...])
    xo_ref[0] = xn
    f_ref[0] = f.astype(f_ref.dtype)


def _ab_out(u, vg, o, ws, bias, w_out, X, mod, gpost, gpre, n_lat):
    B, T, D = X.shape
    tm = TOKEN_TILE
    return pl.pallas_call(
        _ab_out_kernel,
        out_shape=[jax.ShapeDtypeStruct((B, T, D), F32), jax.ShapeDtypeStruct((B, T, D), BF16)],
        grid=(B, pl.cdiv(T, tm)),
        in_specs=[
            _tok_spec(tm, A_WIDTH), _tok_spec(tm, A_WIDTH), _tok_spec(tm, o.shape[2]),
            _const_spec(ws.shape), _const_spec(bias.shape), _const_spec(w_out.shape),
            _tok_spec(tm, D), _mod_spec(n_lat // tm, B, D), _const_spec((1, D)), _const_spec((1, D)),
        ],
        out_specs=[_tok_spec(tm, D), _tok_spec(tm, D)],
        compiler_params=_cparams("arbitrary", "arbitrary"),
        name="ab_out",
    )(u, vg, o, ws, bias, w_out, X, mod, gpost.reshape(1, D), gpre.reshape(1, D))


def _ffn_kernel(f_ref, win_ref, wo_ref, x_ref, mod_ref, gpost_ref, xo_ref):
    dff = wo_ref.shape[0]
    fb = f_ref[0]
    acc = None
    for c0 in range(0, dff, FFN_CHUNK):
        gate = _dot(fb, win_ref[:, c0:c0 + FFN_CHUNK])
        up = _dot(fb, win_ref[:, dff + c0:dff + c0 + FFN_CHUNK])
        part = _dot((_silu(gate) * up).astype(BF16), wo_ref[c0:c0 + FFN_CHUNK, :])
        acc = part if acc is None else acc + part
    xo_ref[0] = x_ref[0] + mod_ref[0][5:6, :] * _rms(acc, gpost_ref[...])


def _ffn(f, w_in, w_out, X, mod, gpost, n_lat):
    B, T, D = X.shape
    tm = TOKEN_TILE
    return pl.pallas_call(
        _ffn_kernel,
        out_shape=jax.ShapeDtypeStruct((B, T, D), F32),
        grid=(B, pl.cdiv(T, tm)),
        in_specs=[
            _tok_spec(tm, D), _const_spec(w_in.shape), _const_spec(w_out.shape),
            _tok_spec(tm, D), _mod_spec(n_lat // tm, B, D), _const_spec((1, D)),
        ],
        out_specs=_tok_spec(tm, D),
        compiler_params=_cparams("arbitrary", "arbitrary"),
        name="ffn",
    )(f, w_in, w_out, X, mod, gpost.reshape(1, D))


def _mla_in_kernel(x_ref, mod_ref, gpre_ref, wdq_ref, qng_ref, wuq_ref, wdkv_ref, kvng_ref, wuk_ref, wuv_ref,
                   cos_ref, slo_ref, shi_ref, q_ref, k_ref, v_ref, *, qscale):
    hb = _modulated(x_ref[0], gpre_ref[...], mod_ref[0], 0, 1).astype(BF16)
    cos = cos_ref[...]
    slo = slo_ref[...]
    shi = shi_ref[...]
    quarter = C_ROPE // 4
    hw = 2 * LANES
    cq = _rms(_dot(hb, wdq_ref[...]), qng_ref[...]).astype(BF16)
    qf = _dot(cq, wuq_ref[...])
    for h in range(C_HEADS):
        q_ref[0, :, h * hw:h * hw + LANES] = (qf[:, h * hw:h * hw + LANES] * qscale).astype(BF16)
        qr = _rope(qf[:, h * hw + LANES:(h + 1) * hw], cos, slo, shi, quarter)
        q_ref[0, :, h * hw + LANES:(h + 1) * hw] = (qr * qscale).astype(BF16)
    kvin = _dot(hb, wdkv_ref[...])
    ckv = _rms(kvin[:, :C_KV_RANK], kvng_ref[...]).astype(BF16)
    krope = _rope(kvin[:, C_KV_RANK:], cos, slo, shi, quarter).astype(BF16)
    kn = _dot(ckv, wuk_ref[...])
    for h in range(C_HEADS):
        k_ref[0, :, h * hw:h * hw + LANES] = kn[:, h * C_NOPE:(h + 1) * C_NOPE].astype(BF16)
        k_ref[0, :, h * hw + LANES:(h + 1) * hw] = krope
    v_ref[0] = _dot(ckv, wuv_ref[...]).astype(BF16)


def _mla_in(X, mod, gpre, wdq, qng, wuq, wdkv, kvng, wuk, wuv, tables, n_lat):
    B, T, D = X.shape
    tm = TOKEN_TILE
    qscale = ((C_NOPE + C_ROPE) ** -0.5) * LOG2E
    hw = 2 * LANES
    return pl.pallas_call(
        functools.partial(_mla_in_kernel, qscale=qscale),
        out_shape=[
            jax.ShapeDtypeStruct((B, T, C_HEADS * hw), BF16),
            jax.ShapeDtypeStruct((B, T, C_HEADS * hw), BF16),
            jax.ShapeDtypeStruct((B, T, C_HEADS * C_V), BF16),
        ],
        grid=(B, pl.cdiv(T, tm)),
        in_specs=[
            _tok_spec(tm, D), _mod_spec(n_lat // tm, B, D), _const_spec((1, D)),
            _const_spec(wdq.shape), _const_spec((1, qng.shape[0])), _const_spec(wuq.shape),
            _const_spec(wdkv.shape), _const_spec((1, kvng.shape[0])), _const_spec(wuk.shape),
            _const_spec(wuv.shape),
            _table_spec(tm), _table_spec(tm), _table_spec(tm),
        ],
        out_specs=[_tok_spec(tm, C_HEADS * hw), _tok_spec(tm, C_HEADS * hw), _tok_spec(tm, C_HEADS * C_V)],
        compiler_params=_cparams("arbitrary", "arbitrary"),
        name="mla_in",
    )(X, mod, gpre.reshape(1, D), wdq, qng.reshape(1, -1), wuq, wdkv, kvng.reshape(1, -1), wuk, wuv, *tables)


def _split_bf16(x):
    hi = x.astype(BF16)
    lo = (x - hi.astype(F32)).astype(BF16)
    return hi, lo


def _mla_out_kernel(o_ref, wo_ref, x_ref, mod_ref, gpost_ref, gpre_ref, rt_ref, xo_ref, f_ref, lg_ref):
    y = _dot(o_ref[0], wo_ref[...])
    xn, f = _post_mix(y, x_ref[0], mod_ref[0], gpost_ref[...], gpre_ref[...])
    xo_ref[0] = xn
    f_ref[0] = f
    f_hi, f_lo = _split_bf16(f)
    r_hi, r_lo = _split_bf16(rt_ref[...])
    lg_ref[0] = _dot_nt(r_hi, f_hi) + (_dot_nt(r_hi, f_lo) + _dot_nt(r_lo, f_hi))


def _mla_out(o, w_o, X, mod, gpost, gpre, router_t, n_lat, n_rows):
    B, _, D = X.shape
    tm = TOKEN_TILE
    ne = router_t.shape[0]
    return pl.pallas_call(
        _mla_out_kernel,
        out_shape=[jax.ShapeDtypeStruct((B, n_rows, D), F32), jax.ShapeDtypeStruct((B, n_rows, D), F32),
                   jax.ShapeDtypeStruct((B, ne, n_rows), F32)],
        grid=(B, pl.cdiv(n_rows, tm)),
        in_specs=[
            _tok_spec(tm, D), _const_spec(w_o.shape), _tok_spec(tm, D), _mod_spec(n_lat // tm, B, D),
            _const_spec((1, D)), _const_spec((1, D)), _const_spec(router_t.shape),
        ],
        out_specs=[_tok_spec(tm, D), _tok_spec(tm, D), pl.BlockSpec((1, ne, tm), lambda b, t: (b, 0, t))],
        compiler_params=_cparams("arbitrary", "arbitrary"),
        name="mla_out",
    )(o, w_o, X, mod, gpost.reshape(1, D), gpre.reshape(1, D), router_t)


def _route_kernel(lg_ref, ints_ref, gates_ref, cnt_ref, carry_ref):
    first = jnp.logical_and(pl.program_id(0) == 0, pl.program_id(1) == 0)

    @pl.when(first)
    def _():
        carry_ref[...] = jnp.zeros(carry_ref.shape, F32)

    lg = lg_ref[0]
    ne, tn = lg.shape
    eidx = lax.broadcasted_iota(jnp.int32, (ne, tn), 0)
    m1 = jnp.max(lg, axis=0, keepdims=True)
    i1 = jnp.min(jnp.where(lg == m1, eidx, ne), axis=0, keepdims=True)
    rest = jnp.where(eidx == i1, -jnp.inf, lg)
    m2 = jnp.max(rest, axis=0, keepdims=True)
    i2 = jnp.min(jnp.where(rest == m2, eidx, ne), axis=0, keepdims=True)
    e2 = jnp.exp(m2 - m1)
    den = 1.0 + e2
    w1 = 1.0 / den
    w2 = e2 / den
    member = jnp.logical_or(eidx == i1, eidx == i2)
    before = lax.broadcasted_iota(jnp.int32, (tn, tn), 0) < lax.broadcasted_iota(jnp.int32, (tn, tn), 1)
    cum = _dot(member.astype(BF16), before.astype(BF16)) + carry_ref[:, 0:1]
    r1 = jnp.sum(jnp.where(eidx == i1, cum, 0.0), axis=0, keepdims=True)
    r2 = jnp.sum(jnp.where(eidx == i2, cum, 0.0), axis=0, keepdims=True)
    carry_ref[...] = carry_ref[...] + jnp.sum(member.astype(F32), axis=1, keepdims=True)
    row = lax.broadcasted_iota(jnp.int32, (8, tn), 0)
    ints = jnp.where(row == 0, i1, jnp.where(row == 1, i2, jnp.where(row == 2, r1.astype(jnp.int32),
                                                                      r2.astype(jnp.int32))))
    ints_ref[0] = ints
    gates_ref[0] = jnp.where(row == 0, w1, w2)
    cnt_ref[...] = carry_ref[...]


def _route(logits_t):
    B, ne, n = logits_t.shape
    tn = ROW_DMA_TILE
    return pl.pallas_call(
        _route_kernel,
        out_shape=[jax.ShapeDtypeStruct((B, 8, n), jnp.int32), jax.ShapeDtypeStruct((B, 8, n), F32),
                   jax.ShapeDtypeStruct((ne, LANES), F32)],
        grid=(B, n // tn),
        in_specs=[pl.BlockSpec((1, ne, tn), lambda b, t: (b, 0, t))],
        out_specs=[pl.BlockSpec((1, 8, tn), lambda b, t: (b, 0, t)),
                   pl.BlockSpec((1, 8, tn), lambda b, t: (b, 0, t)),
                   pl.BlockSpec((ne, LANES), lambda b, t: (0, 0))],
        scratch_shapes=[pltpu.VMEM((ne, LANES), F32)],
        compiler_params=_cparams("arbitrary", "arbitrary"),
        name="moe_route",
    )(logits_t)


def _row_copy(src_ref, src_row, dst_ref, dst_row, sem):
    return pltpu.make_async_copy(src_ref.at[pl.ds(src_row, 1), :], dst_ref.at[pl.ds(dst_row, 1), :], sem)


def _dispatch_kernel(pos_ref, f_ref, xs_in_ref, xs_ref, sem):
    del xs_in_ref
    tm = f_ref.shape[1]
    src = f_ref.at[0]

    def issue(r, carry):
        _row_copy(src, r, xs_ref, pos_ref[0, 0, r], sem.at[0]).start()
        _row_copy(src, r, xs_ref, pos_ref[0, 1, r], sem.at[1]).start()
        return carry

    lax.fori_loop(0, tm, issue, 0, unroll=ROW_DMA_UNROLL)
    for slot in range(2):
        pltpu.make_async_copy(src, xs_ref.at[pl.ds(0, tm), :], sem.at[slot]).wait()


def _dispatch(pos, f, n_sorted):
    B, n, D = f.shape
    tm = ROW_DMA_TILE
    zeros = jnp.zeros((n_sorted, D), f.dtype)
    return pl.pallas_call(
        _dispatch_kernel,
        out_shape=jax.ShapeDtypeStruct((n_sorted, D), f.dtype),
        grid=(B, n // tm),
        in_specs=[
            pl.BlockSpec((1, 2, tm), lambda b, t: (b, 0, t), memory_space=pltpu.SMEM),
            _tok_spec(tm, D),
            pl.BlockSpec(memory_space=pl.ANY),
        ],
        out_specs=pl.BlockSpec(memory_space=pl.ANY),
        scratch_shapes=[pltpu.SemaphoreType.DMA((2,))],
        input_output_aliases={2: 0},
        compiler_params=_cparams("arbitrary", "arbitrary"),
        name="moe_dispatch",
    )(pos, f, zeros)


def _expert_kernel(te_ref, ta_ref, xs_ref, wg_ref, wu_ref, wo_ref, y_ref, xb_ref, acc_ref):
    i = pl.program_id(0)
    j = pl.program_id(1)
    last = pl.num_programs(1) - 1
    active = ta_ref[i] == 1

    @pl.when(jnp.logical_and(active, j == 0))
    def _():
        xb_ref[...] = xs_ref[...].astype(BF16)
        acc_ref[...] = jnp.zeros(acc_ref.shape, F32)

    @pl.when(active)
    def _():
        xb = xb_ref[...]
        acc = acc_ref[...]
        for c in range(wg_ref.shape[2] // EXPERT_FF_CHUNK):
            cs = slice(c * EXPERT_FF_CHUNK, (c + 1) * EXPERT_FF_CHUNK)
            h = (_silu(_dot(xb, wg_ref[0, :, cs])) * _dot(xb, wu_ref[0, :, cs])).astype(BF16)
            acc = acc + _dot(h, wo_ref[0, cs, :])
        acc_ref[...] = acc

    @pl.when(jnp.logical_and(active, j == last))
    def _():
        y_ref[...] = acc_ref[...]

    @pl.when(jnp.logical_and(jnp.logical_not(active), j == last))
    def _():
        y_ref[...] = jnp.zeros(y_ref.shape, F32)


def _experts(tile_expert, tile_active, xs, w_in, w_out):
    n_sorted, D = xs.shape
    tm = EXPERT_TILE
    tf = EXPERT_FF_TILE
    dff = w_out.shape[1]
    nf = dff // tf

    def jj(i, j, ta):
        return jnp.where(ta[i] == 1, j, nf - 1)

    grid_spec = pltpu.PrefetchScalarGridSpec(
        num_scalar_prefetch=2,
        grid=(n_sorted // tm, nf),
        in_specs=[
            pl.BlockSpec((tm, D), lambda i, j, te, ta: (i, 0)),
            pl.BlockSpec((1, D, tf), lambda i, j, te, ta: (te[i], 0, jj(i, j, ta))),
            pl.BlockSpec((1, D, tf), lambda i, j, te, ta: (te[i], 0, nf + jj(i, j, ta))),
            pl.BlockSpec((1, tf, D), lambda i, j, te, ta: (te[i], jj(i, j, ta), 0)),
        ],
        out_specs=pl.BlockSpec((tm, D), lambda i, j, te, ta: (i, 0)),
        scratch_shapes=[pltpu.VMEM((tm, D), BF16), pltpu.VMEM((tm, D), F32)],
    )
    return pl.pallas_call(
        _expert_kernel,
        out_shape=jax.ShapeDtypeStruct((n_sorted, D), F32),
        grid_spec=grid_spec,
        compiler_params=_cparams("arbitrary", "arbitrary"),
        name="moe_experts",
    )(tile_expert, tile_active, xs, w_in, w_in, w_out)


def _combine_kernel(pos_ref, ys_ref, g_ref, x_ref, mod_ref, gpost_ref, xo_ref, buf_a, buf_b, sem):
    tm = x_ref.shape[1]

    def issue(r, carry):
        _row_copy(ys_ref, pos_ref[0, 0, r], buf_a, r, sem.at[0]).start()
        _row_copy(ys_ref, pos_ref[0, 1, r], buf_b, r, sem.at[1]).start()
        return carry

    lax.fori_loop(0, tm, issue, 0, unroll=ROW_DMA_UNROLL)
    pltpu.make_async_copy(ys_ref.at[pl.ds(0, tm), :], buf_a, sem.at[0]).wait()
    pltpu.make_async_copy(ys_ref.at[pl.ds(0, tm), :], buf_b, sem.at[1]).wait()
    g = g_ref[0]
    fo = g[:, 0:1] * buf_a[...] + g[:, 1:2] * buf_b[...]
    xo_ref[0] = x_ref[0] + mod_ref[0][5:6, :] * _rms(fo, gpost_ref[...])


def _combine(pos, ys, gates, X, mod, gpost, n_lat, n_rows):
    B, _, D = X.shape
    tm = ROW_DMA_TILE
    return pl.pallas_call(
        _combine_kernel,
        out_shape=jax.ShapeDtypeStruct((B, n_rows, D), F32),
        grid=(B, n_rows // tm),
        in_specs=[
            pl.BlockSpec((1, 2, tm), lambda b, t: (b, 0, t), memory_space=pltpu.SMEM),
            pl.BlockSpec(memory_space=pl.ANY),
            _tok_spec(tm, 2),
            _tok_spec(tm, D),
            _mod_spec(n_lat // tm, B, D),
            _const_spec((1, D)),
        ],
        out_specs=_tok_spec(tm, D),
        scratch_shapes=[pltpu.VMEM((tm, D), F32), pltpu.VMEM((tm, D), F32), pltpu.SemaphoreType.DMA((2,))],
        compiler_params=_cparams("arbitrary", "arbitrary"),
        name="moe_combine",
    )(pos, ys, gates, X, mod, gpost.reshape(1, D))


def _moe(f, logits_t, w_in, w_out, X, mod, gpost, n_lat):
    B, n, D = f.shape
    ne = logits_t.shape[1]
    tm = EXPERT_TILE
    ints, gates, counts = _route(logits_t)
    counts = counts[:, 0].astype(jnp.int32)
    padded = ((counts + tm - 1) // tm) * tm
    ends = jnp.cumsum(padded)
    starts = ends - padded
    n_tiles = (2 * B * n) // tm + ne
    def start_of(e):
        return sum(jnp.where(e == j, starts[j], 0) for j in range(ne))

    pos = jnp.stack([start_of(ints[:, 0, :]) + ints[:, 2, :], start_of(ints[:, 1, :]) + ints[:, 3, :]], axis=1)
    tile_start = jnp.arange(n_tiles, dtype=jnp.int32) * tm
    tile_active = (tile_start < ends[-1]).astype(jnp.int32)
    tile_expert = jnp.sum((tile_start[:, None] >= ends[None, :]).astype(jnp.int32), axis=1)
    last_expert = jnp.sum((ends[-1] - 1 >= ends).astype(jnp.int32))
    tile_expert = jnp.where(tile_active == 1, tile_expert, last_expert).astype(jnp.int32)
    xs = _dispatch(pos, f, n_tiles * tm)
    ys = _experts(tile_expert, tile_active, xs, w_in, w_out)
    g2 = jnp.transpose(gates[:, 0:2, :], (0, 2, 1))
    return _combine(pos, ys, g2, X, mod, gpost, n_lat, n)


def _rope_tables(n_lat, n_ctx, rot_dim):
    rows = n_lat // GRID_W
    row_idx = jnp.repeat(jnp.arange(rows, dtype=F32), GRID_W)
    col_idx = jnp.tile(jnp.arange(GRID_W, dtype=F32), rows)
    n_freq = rot_dim // 4
    inv_freq = ROPE_THETA ** (-jnp.arange(n_freq, dtype=F32) / n_freq)
    ang = jnp.stack([row_idx[:, None] * inv_freq, col_idx[:, None] * inv_freq], axis=1)
    ang = jnp.broadcast_to(ang[:, :, None, :], (n_lat, 2, 2, n_freq)).reshape(n_lat, rot_dim)
    cos = jnp.cos(ang)
    sin = jnp.sin(ang)
    first = (jnp.arange(rot_dim) % (2 * n_freq)) < n_freq
    sin_lo = jnp.where(first[None, :], -sin, 0.0)
    sin_hi = jnp.where(first[None, :], 0.0, sin)

    def finish(t, ctx_value):
        t = jnp.concatenate([t, jnp.full((n_ctx, rot_dim), ctx_value, F32)], axis=0)
        return jnp.pad(t, ((0, 0), (0, LANES - rot_dim)))

    return finish(cos, 1.0), finish(sin_lo, 0.0), finish(sin_hi, 0.0)


def kernel(x, c, ctx, c_ctx, ada_w, ada_b, g_pre_mix, g_post_mix, g_pre_ffn, g_post_ffn, ab_w_in, a_norm_g, a_ws,
           a_bs, b_qnorm_g, b_knorm_g, ab_w_out, ffn_w_in, ffn_w_out, mla_w_dq, mla_qnorm_g, mla_w_uq, mla_w_dkv,
           mla_kvnorm_g, mla_w_ukv, mla_w_o, moe_router, moe_w_in, moe_w_out):
    B, S, D = x.shape
    NC = ctx.shape[1]
    depth = ada_w.shape[0]
    assert S % TOKEN_TILE == 0 and S % GRID_W == 0 and NC % ROW_DMA_TILE == 0 and TOKEN_TILE % NC == 0

    X = jnp.concatenate([x, ctx], axis=1)
    cpad = jnp.zeros((8, D), F32).at[:B].set(c).at[B].set(c_ctx)
    mods = _adaln(cpad, ada_w, ada_b)
    tables_b = _rope_tables(S, NC, B_HEAD_DIM)
    tables_c = _rope_tables(S, NC, C_ROPE)
    moe_w_out_b = None

    for layer in range(depth):
        need_ctx = layer < depth - 1
        i = layer // 2
        mod = mods[layer]
        n_rows = S + NC if need_ctx else S
        if layer % 2 == 0:
            u, vg, q, k, v = _ab_in(X, mod, g_pre_mix[layer], ab_w_in[i].astype(BF16), a_norm_g[i], b_qnorm_g[i],
                                    b_knorm_g[i], tables_b, S)
            cast = (moe_w_out, i) if layer + 1 < depth else None
            o, moe_w_out_b = _attention(q, k, v, n_lat=S, n_ctx=NC, need_ctx=True, name="gqa",
                                        n_kv_heads=B_KV_HEADS, group=B_HEADS // B_KV_HEADS, dk=B_HEAD_DIM,
                                        dv=B_HEAD_DIM, tq=512, cast_w=cast)
            bias = jnp.broadcast_to(a_bs[i].T[:, :, None], (CHUNK, A_GROUPS, A_GROUP_DIM)).reshape(CHUNK, A_WIDTH)
            X, f = _ab_out(u, vg, o, a_ws[i].astype(BF16), bias, ab_w_out[i].astype(BF16), X, mod,
                           g_post_mix[layer], g_pre_ffn[layer], S)
            X = _ffn(f, ffn_w_in[i].astype(BF16), ffn_w_out[i].astype(BF16), X, mod, g_post_ffn[layer], S)
        else:
            hw = 2 * LANES
            wuq = mla_w_uq[i].reshape(-1, C_HEADS, C_NOPE + C_ROPE)
            wuq = jnp.pad(wuq, ((0, 0), (0, 0), (0, hw - C_NOPE - C_ROPE))).reshape(-1, C_HEADS * hw)
            wdkv = jnp.pad(mla_w_dkv[i], ((0, 0), (0, LANES - C_ROPE)))
            wukv = mla_w_ukv[i].reshape(C_KV_RANK, C_HEADS, C_NOPE + C_V)
            wuk = wukv[:, :, :C_NOPE].reshape(C_KV_RANK, C_HEADS * C_NOPE)
            wuv = wukv[:, :, C_NOPE:].reshape(C_KV_RANK, C_HEADS * C_V)
            q, k, v = _mla_in(X, mod, g_pre_mix[layer], mla_w_dq[i].astype(BF16), mla_qnorm_g[i],
                              wuq.astype(BF16), wdkv.astype(BF16), mla_kvnorm_g[i], wuk.astype(BF16),
                              wuv.astype(BF16), tables_c, S)
            o, moe_w_in_b = _attention(q, k, v, n_lat=S, n_ctx=NC, need_ctx=need_ctx, name="mla",
                                       n_kv_heads=C_HEADS, group=1, dk=hw, dv=C_V, tq=1024,
                                       cast_w=(moe_w_in, i))
            X, f, logits_t = _mla_out(o, mla_w_o[i].astype(BF16), X, mod, g_post_mix[layer], g_pre_ffn[layer],
                                      moe_router[i].T, S, n_rows)
            X = _moe(f, logits_t, moe_w_in_b, moe_w_out_b, X, mod, g_post_ffn[layer], S)
    return X[:, :S] if X.shape[1] != S else X
```

```python
import functools
import math

import jax
import jax.numpy as jnp
from jax import lax
from jax.experimental import pallas as pl
from jax.experimental.pallas import tpu as pltpu

F32 = jnp.float32
BF16 = jnp.bfloat16

GRID_W = 64
ROPE_THETA = 10000.0
EPS = 1e-6
CHUNK = 128
A_GROUPS = 4
A_GROUP_DIM = 128
A_WIDTH = A_GROUPS * A_GROUP_DIM
B_HEADS = 4
B_KV_HEADS = 2
B_HEAD_DIM = 128
C_HEADS = 8
C_KV_RANK = 256
C_NOPE = 128
C_ROPE = 64
C_V = 128
N_EXPERTS = 8
LANES = 128

VMEM_LIMIT_BYTES = 56 * 1024 * 1024
TOKEN_TILE = 512
TOKEN_SUBTILE = 256
ROW_DMA_TILE = 256
FLASH_KV_CHUNK = 2816
EXPERT_TILE = 512
EXPERT_FF_TILE = 1792
EXPERT_FF_CHUNK = 256
ROW_DMA_UNROLL = 8
FFN_CHUNK = 256
LOG2E = math.log2(math.e)


def _cparams(*sem):
    return pltpu.CompilerParams(dimension_semantics=sem, vmem_limit_bytes=VMEM_LIMIT_BYTES)


def _dot(a, b):
    return jnp.dot(a, b, preferred_element_type=F32)


def _dot_nt(a, b):
    return lax.dot_general(a, b, (((1,), (1,)), ((), ())), preferred_element_type=F32)


def _rms(x, g):
    return x * lax.rsqrt(jnp.mean(x * x, axis=-1, keepdims=True) + EPS) * g


def _gelu(x):
    c = math.sqrt(2.0 / math.pi)
    return x * (0.5 * (1.0 + jnp.tanh(c * (x + 0.044715 * (x * x * x)))))


def _silu(x):
    return x * (1.0 / (1.0 + jnp.exp(-x)))


def _rope(x, cos, sin_lo, sin_hi, quarter):
    return (x * cos + pltpu.roll(x, LANES - quarter, 1) * sin_lo
            + pltpu.roll(x, quarter, 1) * sin_hi)


def _modulated(x, g, mod, shift_row, scale_row):
    return (_rms(x, g) * (1.0 + mod[scale_row:scale_row + 1, :])
            + mod[shift_row:shift_row + 1, :])


def _adaln_kernel(c_ref, w_ref, b_ref, o_ref):
    act = _silu(c_ref[...]).astype(BF16)
    o_ref[0, 0] = _dot(act, w_ref[0].astype(BF16)) + b_ref[0, 0]


def _adaln(cpad, ada_w, ada_b):
    depth, d, _ = ada_w.shape
    rows = cpad.shape[0]
    out = pl.pallas_call(
        _adaln_kernel,
        out_shape=jax.ShapeDtypeStruct((depth, 6, rows, d), F32),
        grid=(depth, 6),
        in_specs=[
            pl.BlockSpec((rows, d), lambda l, j: (0, 0)),
            pl.BlockSpec((1, d, d), lambda l, j: (l, 0, j)),
            pl.BlockSpec((1, 1, 1, d), lambda l, j: (l, j, 0, 0)),
        ],
        out_specs=pl.BlockSpec((1, 1, rows, d), lambda l, j: (l, j, 0, 0)),
        compiler_params=_cparams("arbitrary", "arbitrary"),
        name="adaln",
    )(cpad, ada_w, ada_b.reshape(depth, 6, 1, d))
    return jnp.transpose(out, (0, 2, 1, 3))


def _tok_spec(tm, width):
    return pl.BlockSpec((1, tm, width), lambda b, t: (b, t, 0))


def _mod_spec(n_lat_tiles, ctx_row, d):
    return pl.BlockSpec((1, 6, d), lambda b, t: (jnp.where(t >= n_lat_tiles, ctx_row, b), 0, 0))


def _const_spec(shape):
    nd = len(shape)
    return pl.BlockSpec(shape, lambda b, t: (0,) * nd)


def _table_spec(tm):
    return pl.BlockSpec((tm, LANES), lambda b, t: (t, 0))


def _ab_in_kernel(x_ref, mod_ref, gpre_ref, w_ref, ang_ref, qg_ref, kg_ref, cos_ref, slo_ref, shi_ref,
                  u_ref, vg_ref, q_ref, k_ref, v_ref, *, qscale):
    quarter = B_HEAD_DIM // 4
    c0 = 2 * A_WIDTH
    c1 = c0 + B_HEADS * B_HEAD_DIM
    c2 = c1 + B_KV_HEADS * B_HEAD_DIM
    tm = x_ref.shape[1]
    for r0 in range(0, tm, TOKEN_SUBTILE):
        rows = slice(r0, r0 + TOKEN_SUBTILE)
        hb = _modulated(x_ref[0, rows, :], gpre_ref[...], mod_ref[0], 0, 1).astype(BF16)
        cos = cos_ref[rows, :]
        slo = slo_ref[rows, :]
        shi = shi_ref[rows, :]
        u_ref[0, rows, :] = _gelu(_dot(hb, w_ref[:, 0:A_WIDTH])).astype(BF16)
        v = _gelu(_dot(hb, w_ref[:, A_WIDTH:c0]))
        for g in range(A_GROUPS):
            sl = slice(g * A_GROUP_DIM, (g + 1) * A_GROUP_DIM)
            vg_ref[0, rows, sl] = _rms(v[:, sl], ang_ref[:, sl]).astype(BF16)
        q = _dot(hb, w_ref[:, c0:c1])
        for h in range(B_HEADS):
            sl = slice(h * B_HEAD_DIM, (h + 1) * B_HEAD_DIM)
            qr = _rope(_rms(q[:, sl], qg_ref[...]), cos, slo, shi, quarter)
            q_ref[0, rows, sl] = (qr * qscale).astype(BF16)
        k = _dot(hb, w_ref[:, c1:c2])
        for h in range(B_KV_HEADS):
            sl = slice(h * B_HEAD_DIM, (h + 1) * B_HEAD_DIM)
            k_ref[0, rows, sl] = _rope(_rms(k[:, sl], kg_ref[...]), cos, slo, shi, quarter).astype(BF16)
        v_ref[0, rows, :] = _dot(hb, w_ref[:, c2:c2 + B_KV_HEADS * B_HEAD_DIM]).astype(BF16)


def _ab_in(X, mod, gpre, w_in, a_norm_g, qg, kg, tables, n_lat):
    B, T, D = X.shape
    tm = TOKEN_TILE
    qw = B_HEADS * B_HEAD_DIM
    kw = B_KV_HEADS * B_HEAD_DIM
    qscale = (B_HEAD_DIM ** -0.5) * LOG2E
    outs = pl.pallas_call(
        functools.partial(_ab_in_kernel, qscale=qscale),
        out_shape=[
            jax.ShapeDtypeStruct((B, T, A_WIDTH), BF16),
            jax.ShapeDtypeStruct((B, T, A_WIDTH), BF16),
            jax.ShapeDtypeStruct((B, T, qw), BF16),
            jax.ShapeDtypeStruct((B, T, kw), BF16),
            jax.ShapeDtypeStruct((B, T, kw), BF16),
        ],
        grid=(B, pl.cdiv(T, tm)),
        in_specs=[
            _tok_spec(tm, D),
            _mod_spec(n_lat // tm, B, D),
            _const_spec((1, D)),
            _const_spec(w_in.shape),
            _const_spec((1, A_WIDTH)),
            _const_spec((1, B_HEAD_DIM)),
            _const_spec((1, B_HEAD_DIM)),
            _table_spec(tm), _table_spec(tm), _table_spec(tm),
        ],
        out_specs=[_tok_spec(tm, A_WIDTH), _tok_spec(tm, A_WIDTH), _tok_spec(tm, qw),
                   _tok_spec(tm, kw), _tok_spec(tm, kw)],
        compiler_params=_cparams("arbitrary", "arbitrary"),
        name="ab_in",
    )(X, mod, gpre.reshape(1, D), w_in, a_norm_g.reshape(1, A_WIDTH), qg.reshape(1, -1), kg.reshape(1, -1),
      *tables)
    return outs


def _flash_kernel(*refs, group, dk, dv, kv_len, tk, with_cast, with_zeros):
    refs = list(refs)
    q_ref, k_ref, v_ref = refs[:3]
    w_ref = refs[3] if with_cast else None
    outs = refs[3 + with_cast:]
    o_ref = outs.pop(0)
    if with_cast:
        outs.pop(0)[...] = w_ref[...].astype(BF16)
    if with_zeros:
        z_ref = outs.pop(0)
        z_ref[...] = jnp.zeros(z_ref.shape, z_ref.dtype)
    m_ref, l_ref, acc_ref = outs
    tq = q_ref.shape[1]
    if group == 1:
        q = q_ref[0]
    else:
        q = jnp.concatenate([q_ref[0, :, g * dk:(g + 1) * dk] for g in range(group)], axis=0)
    m_ref[...] = jnp.full(m_ref.shape, -jnp.inf, F32)
    l_ref[...] = jnp.zeros(l_ref.shape, F32)
    acc_ref[...] = jnp.zeros(acc_ref.shape, F32)

    for start in range(0, kv_len, tk):
        size = min(tk, kv_len - start)
        s = _dot_nt(q, k_ref[0, start:start + size, :])
        m_prev = m_ref[...]
        m_new = jnp.maximum(m_prev, jnp.max(s, axis=-1, keepdims=True))
        alpha = jnp.exp2(m_prev - m_new)
        p = [jnp.exp2(s[:, t * LANES:(t + 1) * LANES] - m_new) for t in range(size // LANES)]
        l_ref[...] = alpha * l_ref[...] + functools.reduce(lambda a, b: a + b, p)
        pb = jnp.concatenate([t.astype(BF16) for t in p], axis=1)
        acc_ref[...] = alpha * acc_ref[...] + _dot(pb, v_ref[0, start:start + size, :])
        m_ref[...] = m_new

    out = acc_ref[...] * (1.0 / jnp.sum(l_ref[...], axis=-1, keepdims=True))
    for g in range(group):
        o_ref[0, :, g * dv:(g + 1) * dv] = out[g * tq:(g + 1) * tq].astype(o_ref.dtype)


def _flash(q, k, v, *, n_kv_heads, group, dk, dv, tq, q_row0, n_q_rows, kv_row0, kv_len, name, cast_w=None,
           zeros_shape=None):
    B = q.shape[0]
    q_blk0 = q_row0 // tq
    kv_blk0 = kv_row0 // kv_len
    tk = min(FLASH_KV_CHUNK, kv_len)
    n_q = n_q_rows // tq
    n_steps = B * n_kv_heads * n_q

    def step(b, h, i):
        return (b * n_kv_heads + h) * n_q + i

    in_specs = [
        pl.BlockSpec((1, tq, group * dk), lambda b, h, i: (b, q_blk0 + i, h)),
        pl.BlockSpec((1, kv_len, dk), lambda b, h, i: (b, kv_blk0, h)),
        pl.BlockSpec((1, kv_len, dv), lambda b, h, i: (b, kv_blk0, h)),
    ]
    out_specs = [pl.BlockSpec((1, tq, group * dv), lambda b, h, i: (b, i, h))]
    out_shape = [jax.ShapeDtypeStruct((B, n_q_rows, n_kv_heads * group * dv), BF16)]
    args = [q, k, v]
    if cast_w is not None:
        w_all, li = cast_w
        slabs = w_all.reshape(w_all.shape[0] * n_steps, -1, w_all.shape[-1])
        blk = (1,) + slabs.shape[1:]
        in_specs.append(pl.BlockSpec(blk, lambda b, h, i: (li * n_steps + step(b, h, i), 0, 0)))
        out_specs.append(pl.BlockSpec(blk, lambda b, h, i: (step(b, h, i), 0, 0)))
        out_shape.append(jax.ShapeDtypeStruct((n_steps,) + slabs.shape[1:], BF16))
        args.append(slabs)
    if zeros_shape is not None:
        rows, cols = zeros_shape
        out_specs.append(pl.BlockSpec((rows // n_steps, cols), lambda b, h, i: (step(b, h, i), 0)))
        out_shape.append(jax.ShapeDtypeStruct((rows, cols), F32))
    outs = pl.pallas_call(
        functools.partial(_flash_kernel, group=group, dk=dk, dv=dv, kv_len=kv_len, tk=tk,
                          with_cast=cast_w is not None, with_zeros=zeros_shape is not None),
        out_shape=out_shape,
        grid=(B, n_kv_heads, n_q),
        in_specs=in_specs,
        out_specs=out_specs,
        scratch_shapes=[
            pltpu.VMEM((group * tq, LANES), F32),
            pltpu.VMEM((group * tq, LANES), F32),
            pltpu.VMEM((group * tq, dv), F32),
        ],
        compiler_params=_cparams("arbitrary", "arbitrary", "arbitrary"),
        name=name,
    )(*args)
    outs = list(outs)
    o = outs.pop(0)
    w_b = outs.pop(0).reshape(cast_w[0].shape[1:]) if cast_w is not None else None
    zeros = outs.pop(0) if zeros_shape is not None else None
    return o, w_b, zeros


def _attention(q, k, v, *, n_lat, n_ctx, need_ctx, name, cast_w=None, zeros_shape=None, **kw):
    T = n_lat + n_ctx
    o, w_b, zeros = _flash(q, k, v, q_row0=0, n_q_rows=n_lat, kv_row0=0, kv_len=T, name=name + "_lat",
                           cast_w=cast_w, zeros_shape=zeros_shape, **kw)
    if need_ctx:
        kw = dict(kw, tq=min(kw["tq"], n_ctx))
        o_ctx, _, _ = _flash(q, k, v, q_row0=n_lat, n_q_rows=n_ctx, kv_row0=n_lat, kv_len=n_ctx,
                             name=name + "_ctx", **kw)
        o = jnp.concatenate([o, o_ctx], axis=1)
    return o, w_b, zeros


def _post_mix(y, x, mod, gpost, gpre):
    xn = x + mod[2:3, :] * _rms(y, gpost)
    return xn, _modulated(xn, gpre, mod, 3, 4)


def _ab_out_kernel(u_ref, vg_ref, o_ref, ws_ref, bias_ref, wout_ref, x_ref, mod_ref, gpost_ref, gpre_ref,
                   xo_ref, f_ref):
    tm = u_ref.shape[1]
    y = _dot(o_ref[0], wout_ref[A_WIDTH:, :])
    gated = []
    for c in range(tm // CHUNK):
        rows = slice(c * CHUNK, (c + 1) * CHUNK)
        mixed = jnp.concatenate(
            [_dot(ws_ref[g], vg_ref[0, rows, g * A_GROUP_DIM:(g + 1) * A_GROUP_DIM]) for g in range(A_GROUPS)],
            axis=1) + bias_ref[...]
        gated.append((u_ref[0, rows, :].astype(F32) * mixed).astype(BF16))
    y = y + _dot(jnp.concatenate(gated, axis=0), wout_ref[:A_WIDTH, :])
    xn, f = _post_mix(y, x_ref[0], mod_ref[0], gpost_ref[...], gpre_ref[...])
    xo_ref[0] = xn
    f_ref[0] = f.astype(f_ref.dtype)


def _ab_out(u, vg, o, ws, bias, w_out, X, mod, gpost, gpre, n_lat):
    B, T, D = X.shape
    tm = TOKEN_TILE
    return pl.pallas_call(
        _ab_out_kernel,
        out_shape=[jax.ShapeDtypeStruct((B, T, D), F32), jax.ShapeDtypeStruct((B, T, D), BF16)],
        grid=(B, pl.cdiv(T, tm)),
        in_specs=[
            _tok_spec(tm, A_WIDTH), _tok_spec(tm, A_WIDTH), _tok_spec(tm, o.shape[2]),
            _const_spec(ws.shape), _const_spec(bias.shape), _const_spec(w_out.shape),
            _tok_spec(tm, D), _mod_spec(n_lat // tm, B, D), _const_spec((1, D)), _const_spec((1, D)),
        ],
        out_specs=[_tok_spec(tm, D), _tok_spec(tm, D)],
        compiler_params=_cparams("arbitrary", "arbitrary"),
        name="ab_out",
    )(u, vg, o, ws, bias, w_out, X, mod, gpost.reshape(1, D), gpre.reshape(1, D))


def _ffn_kernel(f_ref, win_ref, wo_ref, x_ref, mod_ref, gpost_ref, xo_ref):
    dff = wo_ref.shape[0]
    fb = f_ref[0]
    acc = None
    for c0 in range(0, dff, FFN_CHUNK):
        gate = _dot(fb, win_ref[:, c0:c0 + FFN_CHUNK])
        up = _dot(fb, win_ref[:, dff + c0:dff + c0 + FFN_CHUNK])
        part = _dot((_silu(gate) * up).astype(BF16), wo_ref[c0:c0 + FFN_CHUNK, :])
        acc = part if acc is None else acc + part
    xo_ref[0] = x_ref[0] + mod_ref[0][5:6, :] * _rms(acc, gpost_ref[...])


def _ffn(f, w_in, w_out, X, mod, gpost, n_lat):
    B, T, D = X.shape
    tm = TOKEN_TILE
    return pl.pallas_call(
        _ffn_kernel,
        out_shape=jax.ShapeDtypeStruct((B, T, D), F32),
        grid=(B, pl.cdiv(T, tm)),
        in_specs=[
            _tok_spec(tm, D), _const_spec(w_in.shape), _const_spec(w_out.shape),
            _tok_spec(tm, D), _mod_spec(n_lat // tm, B, D), _const_spec((1, D)),
        ],
        out_specs=_tok_spec(tm, D),
        compiler_params=_cparams("arbitrary", "arbitrary"),
        name="ffn",
    )(f, w_in, w_out, X, mod, gpost.reshape(1, D))


def _mla_in_kernel(x_ref, mod_ref, gpre_ref, wdq_ref, qng_ref, wuq_ref, wdkv_ref, kvng_ref, wuk_ref, wuv_ref,
                   cos_ref, slo_ref, shi_ref, q_ref, k_ref, v_ref, *, qscale):
    hb = _modulated(x_ref[0], gpre_ref[...], mod_ref[0], 0, 1).astype(BF16)
    cos = cos_ref[...]
    slo = slo_ref[...]
    shi = shi_ref[...]
    quarter = C_ROPE // 4
    hw = 2 * LANES
    cq = _rms(_dot(hb, wdq_ref[...]), qng_ref[...]).astype(BF16)
    qf = _dot(cq, wuq_ref[...])
    for h in range(C_HEADS):
        q_ref[0, :, h * hw:h * hw + LANES] = (qf[:, h * hw:h * hw + LANES] * qscale).astype(BF16)
        qr = _rope(qf[:, h * hw + LANES:(h + 1) * hw], cos, slo, shi, quarter)
        q_ref[0, :, h * hw + LANES:(h + 1) * hw] = (qr * qscale).astype(BF16)
    kvin = _dot(hb, wdkv_ref[...])
    ckv = _rms(kvin[:, :C_KV_RANK], kvng_ref[...]).astype(BF16)
    krope = _rope(kvin[:, C_KV_RANK:], cos, slo, shi, quarter).astype(BF16)
    kn = _dot(ckv, wuk_ref[...])
    for h in range(C_HEADS):
        k_ref[0, :, h * hw:h * hw + LANES] = kn[:, h * C_NOPE:(h + 1) * C_NOPE].astype(BF16)
        k_ref[0, :, h * hw + LANES:(h + 1) * hw] = krope
    v_ref[0] = _dot(ckv, wuv_ref[...]).astype(BF16)


def _mla_in(X, mod, gpre, wdq, qng, wuq, wdkv, kvng, wuk, wuv, tables, n_lat):
    B, T, D = X.shape
    tm = TOKEN_TILE
    qscale = ((C_NOPE + C_ROPE) ** -0.5) * LOG2E
    hw = 2 * LANES
    return pl.pallas_call(
        functools.partial(_mla_in_kernel, qscale=qscale),
        out_shape=[
            jax.ShapeDtypeStruct((B, T, C_HEADS * hw), BF16),
            jax.ShapeDtypeStruct((B, T, C_HEADS * hw), BF16),
            jax.ShapeDtypeStruct((B, T, C_HEADS * C_V), BF16),
        ],
        grid=(B, pl.cdiv(T, tm)),
        in_specs=[
            _tok_spec(tm, D), _mod_spec(n_lat // tm, B, D), _const_spec((1, D)),
            _const_spec(wdq.shape), _const_spec((1, qng.shape[0])), _const_spec(wuq.shape),
            _const_spec(wdkv.shape), _const_spec((1, kvng.shape[0])), _const_spec(wuk.shape),
            _const_spec(wuv.shape),
            _table_spec(tm), _table_spec(tm), _table_spec(tm),
        ],
        out_specs=[_tok_spec(tm, C_HEADS * hw), _tok_spec(tm, C_HEADS * hw), _tok_spec(tm, C_HEADS * C_V)],
        compiler_params=_cparams("arbitrary", "arbitrary"),
        name="mla_in",
    )(X, mod, gpre.reshape(1, D), wdq, qng.reshape(1, -1), wuq, wdkv, kvng.reshape(1, -1), wuk, wuv, *tables)


def _split_bf16(x):
    hi = x.astype(BF16)
    lo = (x - hi.astype(F32)).astype(BF16)
    return hi, lo


def _mla_out_kernel(o_ref, wo_ref, x_ref, mod_ref, gpost_ref, gpre_ref, rt_ref, xo_ref, f_ref, lg_ref):
    y = _dot(o_ref[0], wo_ref[...])
    xn, f = _post_mix(y, x_ref[0], mod_ref[0], gpost_ref[...], gpre_ref[...])
    xo_ref[0] = xn
    f_ref[0] = f
    f_hi, f_lo = _split_bf16(f)
    r_hi, r_lo = _split_bf16(rt_ref[...])
    lg_ref[0] = _dot_nt(r_hi, f_hi) + (_dot_nt(r_hi, f_lo) + _dot_nt(r_lo, f_hi))


def _mla_out(o, w_o, X, mod, gpost, gpre, router_t, n_lat, n_rows):
    B, _, D = X.shape
    tm = TOKEN_TILE
    ne = router_t.shape[0]
    return pl.pallas_call(
        _mla_out_kernel,
        out_shape=[jax.ShapeDtypeStruct((B, n_rows, D), F32), jax.ShapeDtypeStruct((B, n_rows, D), F32),
                   jax.ShapeDtypeStruct((B, ne, n_rows), F32)],
        grid=(B, pl.cdiv(n_rows, tm)),
        in_specs=[
            _tok_spec(tm, D), _const_spec(w_o.shape), _tok_spec(tm, D), _mod_spec(n_lat // tm, B, D),
            _const_spec((1, D)), _const_spec((1, D)), _const_spec(router_t.shape),
        ],
        out_specs=[_tok_spec(tm, D), _tok_spec(tm, D), pl.BlockSpec((1, ne, tm), lambda b, t: (b, 0, t))],
        compiler_params=_cparams("arbitrary", "arbitrary"),
        name="mla_out",
    )(o, w_o, X, mod, gpost.reshape(1, D), gpre.reshape(1, D), router_t)


def _route_kernel(lg_ref, ints_ref, gates_ref, cnt_ref, carry_ref):
    first = jnp.logical_and(pl.program_id(0) == 0, pl.program_id(1) == 0)

    @pl.when(first)
    def _():
        carry_ref[...] = jnp.zeros(carry_ref.shape, F32)

    lg = lg_ref[0]
    ne, tn = lg.shape
    eidx = lax.broadcasted_iota(jnp.int32, (ne, tn), 0)
    m1 = jnp.max(lg, axis=0, keepdims=True)
    i1 = jnp.min(jnp.where(lg == m1, eidx, ne), axis=0, keepdims=True)
    rest = jnp.where(eidx == i1, -jnp.inf, lg)
    m2 = jnp.max(rest, axis=0, keepdims=True)
    i2 = jnp.min(jnp.where(rest == m2, eidx, ne), axis=0, keepdims=True)
    e2 = jnp.exp(m2 - m1)
    den = 1.0 + e2
    w1 = 1.0 / den
    w2 = e2 / den
    member = jnp.logical_or(eidx == i1, eidx == i2)
    before = lax.broadcasted_iota(jnp.int32, (tn, tn), 0) < lax.broadcasted_iota(jnp.int32, (tn, tn), 1)
    cum = _dot(member.astype(BF16), before.astype(BF16)) + carry_ref[:, 0:1]
    r1 = jnp.sum(jnp.where(eidx == i1, cum, 0.0), axis=0, keepdims=True)
    r2 = jnp.sum(jnp.where(eidx == i2, cum, 0.0), axis=0, keepdims=True)
    carry_ref[...] = carry_ref[...] + jnp.sum(member.astype(F32), axis=1, keepdims=True)
    row = lax.broadcasted_iota(jnp.int32, (8, tn), 0)
    ints = jnp.where(row == 0, i1, jnp.where(row == 1, i2, jnp.where(row == 2, r1.astype(jnp.int32),
                                                                      r2.astype(jnp.int32))))
    ints_ref[0] = ints
    gates_ref[0] = jnp.where(row == 0, w1, w2)
    cnt_ref[...] = carry_ref[...]


def _route(logits_t):
    B, ne, n = logits_t.shape
    tn = ROW_DMA_TILE
    return pl.pallas_call(
        _route_kernel,
        out_shape=[jax.ShapeDtypeStruct((B, 8, n), jnp.int32), jax.ShapeDtypeStruct((B, 8, n), F32),
                   jax.ShapeDtypeStruct((ne, LANES), F32)],
        grid=(B, n // tn),
        in_specs=[pl.BlockSpec((1, ne, tn), lambda b, t: (b, 0, t))],
        out_specs=[pl.BlockSpec((1, 8, tn), lambda b, t: (b, 0, t)),
                   pl.BlockSpec((1, 8, tn), lambda b, t: (b, 0, t)),
                   pl.BlockSpec((ne, LANES), lambda b, t: (0, 0))],
        scratch_shapes=[pltpu.VMEM((ne, LANES), F32)],
        compiler_params=_cparams("arbitrary", "arbitrary"),
        name="moe_route",
    )(logits_t)


def _row_copy(src_ref, src_row, dst_ref, dst_row, sem):
    return pltpu.make_async_copy(src_ref.at[pl.ds(src_row, 1), :], dst_ref.at[pl.ds(dst_row, 1), :], sem)


def _dispatch_kernel(pos_ref, f_ref, xs_in_ref, xs_ref, sem):
    del xs_in_ref
    tm = f_ref.shape[1]
    src = f_ref.at[0]

    def issue(r, carry):
        _row_copy(src, r, xs_ref, pos_ref[0, 0, r], sem.at[0]).start()
        _row_copy(src, r, xs_ref, pos_ref[0, 1, r], sem.at[1]).start()
        return carry

    lax.fori_loop(0, tm, issue, 0, unroll=ROW_DMA_UNROLL)
    for slot in range(2):
        pltpu.make_async_copy(src, xs_ref.at[pl.ds(0, tm), :], sem.at[slot]).wait()


def _dispatch(pos, f, zeros):
    B, n, D = f.shape
    n_sorted = zeros.shape[0]
    tm = ROW_DMA_TILE
    return pl.pallas_call(
        _dispatch_kernel,
        out_shape=jax.ShapeDtypeStruct((n_sorted, D), f.dtype),
        grid=(B, n // tm),
        in_specs=[
            pl.BlockSpec((1, 2, tm), lambda b, t: (b, 0, t), memory_space=pltpu.SMEM),
            _tok_spec(tm, D),
            pl.BlockSpec(memory_space=pl.ANY),
        ],
        out_specs=pl.BlockSpec(memory_space=pl.ANY),
        scratch_shapes=[pltpu.SemaphoreType.DMA((2,))],
        input_output_aliases={2: 0},
        compiler_params=_cparams("arbitrary", "arbitrary"),
        name="moe_dispatch",
    )(pos, f, zeros)


def _expert_kernel(te_ref, ta_ref, xs_ref, wg_ref, wu_ref, wo_ref, y_ref, xb_ref, acc_ref):
    i = pl.program_id(0)
    j = pl.program_id(1)
    last = pl.num_programs(1) - 1
    active = ta_ref[i] == 1

    @pl.when(jnp.logical_and(active, j == 0))
    def _():
        xb_ref[...] = xs_ref[...].astype(BF16)
        acc_ref[...] = jnp.zeros(acc_ref.shape, F32)

    @pl.when(active)
    def _():
        xb = xb_ref[...]
        acc = acc_ref[...]
        for c in range(wg_ref.shape[2] // EXPERT_FF_CHUNK):
            cs = slice(c * EXPERT_FF_CHUNK, (c + 1) * EXPERT_FF_CHUNK)
            h = (_silu(_dot(xb, wg_ref[0, :, cs])) * _dot(xb, wu_ref[0, :, cs])).astype(BF16)
            acc = acc + _dot(h, wo_ref[0, cs, :])
        acc_ref[...] = acc

    @pl.when(jnp.logical_and(active, j == last))
    def _():
        y_ref[...] = acc_ref[...]

    @pl.when(jnp.logical_and(jnp.logical_not(active), j == last))
    def _():
        y_ref[...] = jnp.zeros(y_ref.shape, F32)


def _experts(tile_expert, tile_active, xs, w_in, w_out):
    n_sorted, D = xs.shape
    tm = EXPERT_TILE
    tf = EXPERT_FF_TILE
    dff = w_out.shape[1]
    nf = dff // tf

    def jj(i, j, ta):
        return jnp.where(ta[i] == 1, j, nf - 1)

    grid_spec = pltpu.PrefetchScalarGridSpec(
        num_scalar_prefetch=2,
        grid=(n_sorted // tm, nf),
        in_specs=[
            pl.BlockSpec((tm, D), lambda i, j, te, ta: (i, 0)),
            pl.BlockSpec((1, D, tf), lambda i, j, te, ta: (te[i], 0, jj(i, j, ta))),
            pl.BlockSpec((1, D, tf), lambda i, j, te, ta: (te[i], 0, nf + jj(i, j, ta))),
            pl.BlockSpec((1, tf, D), lambda i, j, te, ta: (te[i], jj(i, j, ta), 0)),
        ],
        out_specs=pl.BlockSpec((tm, D), lambda i, j, te, ta: (i, 0)),
        scratch_shapes=[pltpu.VMEM((tm, D), BF16), pltpu.VMEM((tm, D), F32)],
    )
    return pl.pallas_call(
        _expert_kernel,
        out_shape=jax.ShapeDtypeStruct((n_sorted, D), F32),
        grid_spec=grid_spec,
        compiler_params=_cparams("arbitrary", "arbitrary"),
        name="moe_experts",
    )(tile_expert, tile_active, xs, w_in, w_in, w_out)


def _combine_kernel(pos_ref, ys_ref, g_ref, x_ref, mod_ref, gpost_ref, xo_ref, buf_a, buf_b, sem):
    tm = x_ref.shape[1]

    def issue(r, carry):
        _row_copy(ys_ref, pos_ref[0, 0, r], buf_a, r, sem.at[0]).start()
        _row_copy(ys_ref, pos_ref[0, 1, r], buf_b, r, sem.at[1]).start()
        return carry

    lax.fori_loop(0, tm, issue, 0, unroll=ROW_DMA_UNROLL)
    pltpu.make_async_copy(ys_ref.at[pl.ds(0, tm), :], buf_a, sem.at[0]).wait()
    pltpu.make_async_copy(ys_ref.at[pl.ds(0, tm), :], buf_b, sem.at[1]).wait()
    g = g_ref[0]
    fo = g[:, 0:1] * buf_a[...] + g[:, 1:2] * buf_b[...]
    xo_ref[0] = x_ref[0] + mod_ref[0][5:6, :] * _rms(fo, gpost_ref[...])


def _combine(pos, ys, gates, X, mod, gpost, n_lat, n_rows):
    B, _, D = X.shape
    tm = ROW_DMA_TILE
    return pl.pallas_call(
        _combine_kernel,
        out_shape=jax.ShapeDtypeStruct((B, n_rows, D), F32),
        grid=(B, n_rows // tm),
        in_specs=[
            pl.BlockSpec((1, 2, tm), lambda b, t: (b, 0, t), memory_space=pltpu.SMEM),
            pl.BlockSpec(memory_space=pl.ANY),
            _tok_spec(tm, 2),
            _tok_spec(tm, D),
            _mod_spec(n_lat // tm, B, D),
            _const_spec((1, D)),
        ],
        out_specs=_tok_spec(tm, D),
        scratch_shapes=[pltpu.VMEM((tm, D), F32), pltpu.VMEM((tm, D), F32), pltpu.SemaphoreType.DMA((2,))],
        compiler_params=_cparams("arbitrary", "arbitrary"),
        name="moe_combine",
    )(pos, ys, gates, X, mod, gpost.reshape(1, D))


def _n_expert_tiles(n_tokens, n_experts):
    return (2 * n_tokens) // EXPERT_TILE + n_experts


def _moe(f, logits_t, w_in, w_out, xs_zeros, X, mod, gpost, n_lat):
    B, n, D = f.shape
    ne = logits_t.shape[1]
    tm = EXPERT_TILE
    ints, gates, counts = _route(logits_t)
    counts = counts[:, 0].astype(jnp.int32)
    padded = ((counts + tm - 1) // tm) * tm
    ends = jnp.cumsum(padded)
    starts = ends - padded
    n_tiles = _n_expert_tiles(B * n, ne)
    def start_of(e):
        return sum(jnp.where(e == j, starts[j], 0) for j in range(ne))

    pos = jnp.stack([start_of(ints[:, 0, :]) + ints[:, 2, :], start_of(ints[:, 1, :]) + ints[:, 3, :]], axis=1)
    tile_start = jnp.arange(n_tiles, dtype=jnp.int32) * tm
    tile_active = (tile_start < ends[-1]).astype(jnp.int32)
    tile_expert = jnp.sum((tile_start[:, None] >= ends[None, :]).astype(jnp.int32), axis=1)
    last_expert = jnp.sum((ends[-1] - 1 >= ends).astype(jnp.int32))
    tile_expert = jnp.where(tile_active == 1, tile_expert, last_expert).astype(jnp.int32)
    xs = _dispatch(pos, f, xs_zeros)
    ys = _experts(tile_expert, tile_active, xs, w_in, w_out)
    g2 = jnp.transpose(gates[:, 0:2, :], (0, 2, 1))
    return _combine(pos, ys, g2, X, mod, gpost, n_lat, n)


def _rope_tables(n_lat, n_ctx, rot_dim):
    rows = n_lat // GRID_W
    row_idx = jnp.repeat(jnp.arange(rows, dtype=F32), GRID_W)
    col_idx = jnp.tile(jnp.arange(GRID_W, dtype=F32), rows)
    n_freq = rot_dim // 4
    inv_freq = ROPE_THETA ** (-jnp.arange(n_freq, dtype=F32) / n_freq)
    ang = jnp.stack([row_idx[:, None] * inv_freq, col_idx[:, None] * inv_freq], axis=1)
    ang = jnp.broadcast_to(ang[:, :, None, :], (n_lat, 2, 2, n_freq)).reshape(n_lat, rot_dim)
    cos = jnp.cos(ang)
    sin = jnp.sin(ang)
    first = (jnp.arange(rot_dim) % (2 * n_freq)) < n_freq
    sin_lo = jnp.where(first[None, :], -sin, 0.0)
    sin_hi = jnp.where(first[None, :], 0.0, sin)

    def finish(t, ctx_value):
        t = jnp.concatenate([t, jnp.full((n_ctx, rot_dim), ctx_value, F32)], axis=0)
        return jnp.pad(t, ((0, 0), (0, LANES - rot_dim)))

    return finish(cos, 1.0), finish(sin_lo, 0.0), finish(sin_hi, 0.0)


def kernel(x, c, ctx, c_ctx, ada_w, ada_b, g_pre_mix, g_post_mix, g_pre_ffn, g_post_ffn, ab_w_in, a_norm_g, a_ws,
           a_bs, b_qnorm_g, b_knorm_g, ab_w_out, ffn_w_in, ffn_w_out, mla_w_dq, mla_qnorm_g, mla_w_uq, mla_w_dkv,
           mla_kvnorm_g, mla_w_ukv, mla_w_o, moe_router, moe_w_in, moe_w_out):
    B, S, D = x.shape
    NC = ctx.shape[1]
    depth = ada_w.shape[0]
    assert S % TOKEN_TILE == 0 and S % GRID_W == 0 and NC % ROW_DMA_TILE == 0 and TOKEN_TILE % NC == 0

    X = jnp.concatenate([x, ctx], axis=1)
    cpad = jnp.zeros((8, D), F32).at[:B].set(c).at[B].set(c_ctx)
    mods = _adaln(cpad, ada_w, ada_b)
    tables_b = _rope_tables(S, NC, B_HEAD_DIM)
    tables_c = _rope_tables(S, NC, C_ROPE)
    moe_w_out_b = None

    for layer in range(depth):
        need_ctx = layer < depth - 1
        i = layer // 2
        mod = mods[layer]
        n_rows = S + NC if need_ctx else S
        if layer % 2 == 0:
            u, vg, q, k, v = _ab_in(X, mod, g_pre_mix[layer], ab_w_in[i].astype(BF16), a_norm_g[i], b_qnorm_g[i],
                                    b_knorm_g[i], tables_b, S)
            cast = (moe_w_out, i) if layer + 1 < depth else None
            o, moe_w_out_b, _ = _attention(q, k, v, n_lat=S, n_ctx=NC, need_ctx=True, name="gqa",
                                        n_kv_heads=B_KV_HEADS, group=B_HEADS // B_KV_HEADS, dk=B_HEAD_DIM,
                                        dv=B_HEAD_DIM, tq=512, cast_w=cast)
            bias = jnp.broadcast_to(a_bs[i].T[:, :, None], (CHUNK, A_GROUPS, A_GROUP_DIM)).reshape(CHUNK, A_WIDTH)
            X, f = _ab_out(u, vg, o, a_ws[i].astype(BF16), bias, ab_w_out[i].astype(BF16), X, mod,
                           g_post_mix[layer], g_pre_ffn[layer], S)
            X = _ffn(f, ffn_w_in[i].astype(BF16), ffn_w_out[i].astype(BF16), X, mod, g_post_ffn[layer], S)
        else:
            hw = 2 * LANES
            wuq = mla_w_uq[i].reshape(-1, C_HEADS, C_NOPE + C_ROPE)
            wuq = jnp.pad(wuq, ((0, 0), (0, 0), (0, hw - C_NOPE - C_ROPE))).reshape(-1, C_HEADS * hw)
            wdkv = jnp.pad(mla_w_dkv[i], ((0, 0), (0, LANES - C_ROPE)))
            wukv = mla_w_ukv[i].reshape(C_KV_RANK, C_HEADS, C_NOPE + C_V)
            wuk = wukv[:, :, :C_NOPE].reshape(C_KV_RANK, C_HEADS * C_NOPE)
            wuv = wukv[:, :, C_NOPE:].reshape(C_KV_RANK, C_HEADS * C_V)
            q, k, v = _mla_in(X, mod, g_pre_mix[layer], mla_w_dq[i].astype(BF16), mla_qnorm_g[i],
                              wuq.astype(BF16), wdkv.astype(BF16), mla_kvnorm_g[i], wuk.astype(BF16),
                              wuv.astype(BF16), tables_c, S)
            n_sorted = _n_expert_tiles(B * n_rows, moe_router.shape[-1]) * EXPERT_TILE
            o, moe_w_in_b, xs_zeros = _attention(q, k, v, n_lat=S, n_ctx=NC, need_ctx=need_ctx, name="mla",
                                                 n_kv_heads=C_HEADS, group=1, dk=hw, dv=C_V, tq=1024,
                                                 cast_w=(moe_w_in, i), zeros_shape=(n_sorted, D))
            X, f, logits_t = _mla_out(o, mla_w_o[i].astype(BF16), X, mod, g_post_mix[layer], g_pre_ffn[layer],
                                      moe_router[i].T, S, n_rows)
            X = _moe(f, logits_t, moe_w_in_b, moe_w_out_b, xs_zeros, X, mod, g_post_ffn[layer], S)
    return X[:, :S] if X.shape[1] != S else X
```

```python
import functools
import math

import jax
import jax.numpy as jnp
from jax import lax
from jax.experimental import pallas as pl
from jax.experimental.pallas import tpu as pltpu

F32 = jnp.float32
BF16 = jnp.bfloat16

GRID_W = 64
ROPE_THETA = 10000.0
EPS = 1e-6
CHUNK = 128
A_GROUPS = 4
A_GROUP_DIM = 128
A_WIDTH = A_GROUPS * A_GROUP_DIM
B_HEADS = 4
B_KV_HEADS = 2
B_HEAD_DIM = 128
C_HEADS = 8
C_KV_RANK = 256
C_NOPE = 128
C_ROPE = 64
C_V = 128
N_EXPERTS = 8
LANES = 128

VMEM_LIMIT_BYTES = 56 * 1024 * 1024
TOKEN_TILE = 512
TOKEN_SUBTILE = 256
ROW_DMA_TILE = 256
FLASH_KV_CHUNK = 2816
EXPERT_TILE = 512
EXPERT_FF_TILE = 1792
EXPERT_FF_CHUNK = 256
ROW_DMA_UNROLL = 8
FFN_CHUNK = 256
LOG2E = math.log2(math.e)


def _cparams(*sem):
    return pltpu.CompilerParams(dimension_semantics=sem, vmem_limit_bytes=VMEM_LIMIT_BYTES)


def _dot(a, b):
    return jnp.dot(a, b, preferred_element_type=F32)


def _dot_nt(a, b):
    return lax.dot_general(a, b, (((1,), (1,)), ((), ())), preferred_element_type=F32)


def _rms(x, g):
    return x * lax.rsqrt(jnp.mean(x * x, axis=-1, keepdims=True) + EPS) * g


def _gelu(x):
    c = math.sqrt(2.0 / math.pi)
    return x * (0.5 * (1.0 + jnp.tanh(c * (x + 0.044715 * (x * x * x)))))


def _silu(x):
    return x * (1.0 / (1.0 + jnp.exp(-x)))


def _rope(x, cos, sin_lo, sin_hi, quarter):
    return (x * cos + pltpu.roll(x, LANES - quarter, 1) * sin_lo
            + pltpu.roll(x, quarter, 1) * sin_hi)


def _modulated(x, g, mod, shift_row, scale_row):
    return (_rms(x, g) * (1.0 + mod[scale_row:scale_row + 1, :])
            + mod[shift_row:shift_row + 1, :])


def _adaln_kernel(c_ref, w_ref, b_ref, o_ref):
    act = _silu(c_ref[...]).astype(BF16)
    o_ref[0, 0] = _dot(act, w_ref[0].astype(BF16)) + b_ref[0, 0]


def _adaln(cpad, ada_w, ada_b):
    depth, d, _ = ada_w.shape
    rows = cpad.shape[0]
    out = pl.pallas_call(
        _adaln_kernel,
        out_shape=jax.ShapeDtypeStruct((depth, 6, rows, d), F32),
        grid=(depth, 6),
        in_specs=[
            pl.BlockSpec((rows, d), lambda l, j: (0, 0)),
            pl.BlockSpec((1, d, d), lambda l, j: (l, 0, j)),
            pl.BlockSpec((1, 1, 1, d), lambda l, j: (l, j, 0, 0)),
        ],
        out_specs=pl.BlockSpec((1, 1, rows, d), lambda l, j: (l, j, 0, 0)),
        compiler_params=_cparams("arbitrary", "arbitrary"),
        name="adaln",
    )(cpad, ada_w, ada_b.reshape(depth, 6, 1, d))
    return jnp.transpose(out, (0, 2, 1, 3))


def _tok_spec(tm, width):
    return pl.BlockSpec((1, tm, width), lambda b, t: (b, t, 0))


def _mod_spec(n_lat_tiles, ctx_row, d):
    return pl.BlockSpec((1, 6, d), lambda b, t: (jnp.where(t >= n_lat_tiles, ctx_row, b), 0, 0))


def _const_spec(shape):
    nd = len(shape)
    return pl.BlockSpec(shape, lambda b, t: (0,) * nd)


def _table_spec(tm):
    return pl.BlockSpec((tm, LANES), lambda b, t: (t, 0))


def _ab_in_kernel(x_ref, mod_ref, gpre_ref, w_ref, ang_ref, qg_ref, kg_ref, cos_ref, slo_ref, shi_ref,
                  u_ref, vg_ref, q_ref, k_ref, v_ref, *, qscale):
    quarter = B_HEAD_DIM // 4
    c0 = 2 * A_WIDTH
    c1 = c0 + B_HEADS * B_HEAD_DIM
    c2 = c1 + B_KV_HEADS * B_HEAD_DIM
    tm = x_ref.shape[1]
    for r0 in range(0, tm, TOKEN_SUBTILE):
        rows = slice(r0, r0 + TOKEN_SUBTILE)
        hb = _modulated(x_ref[0, rows, :], gpre_ref[...], mod_ref[0], 0, 1).astype(BF16)
        cos = cos_ref[rows, :]
        slo = slo_ref[rows, :]
        shi = shi_ref[rows, :]
        u_ref[0, rows, :] = _gelu(_dot(hb, w_ref[:, 0:A_WIDTH])).astype(BF16)
        v = _gelu(_dot(hb, w_ref[:, A_WIDTH:c0]))
        for g in range(A_GROUPS):
            sl = slice(g * A_GROUP_DIM, (g + 1) * A_GROUP_DIM)
            vg_ref[0, rows, sl] = _rms(v[:, sl], ang_ref[:, sl]).astype(BF16)
        q = _dot(hb, w_ref[:, c0:c1])
        for h in range(B_HEADS):
            sl = slice(h * B_HEAD_DIM, (h + 1) * B_HEAD_DIM)
            qr = _rope(_rms(q[:, sl], qg_ref[...]), cos, slo, shi, quarter)
            q_ref[0, rows, sl] = (qr * qscale).astype(BF16)
        k = _dot(hb, w_ref[:, c1:c2])
        for h in range(B_KV_HEADS):
            sl = slice(h * B_HEAD_DIM, (h + 1) * B_HEAD_DIM)
            k_ref[0, rows, sl] = _rope(_rms(k[:, sl], kg_ref[...]), cos, slo, shi, quarter).astype(BF16)
        v_ref[0, rows, :] = _dot(hb, w_ref[:, c2:c2 + B_KV_HEADS * B_HEAD_DIM]).astype(BF16)


def _ab_in(X, mod, gpre, w_in, a_norm_g, qg, kg, tables, n_lat):
    B, T, D = X.shape
    tm = TOKEN_TILE
    qw = B_HEADS * B_HEAD_DIM
    kw = B_KV_HEADS * B_HEAD_DIM
    qscale = (B_HEAD_DIM ** -0.5) * LOG2E
    outs = pl.pallas_call(
        functools.partial(_ab_in_kernel, qscale=qscale),
        out_shape=[
            jax.ShapeDtypeStruct((B, T, A_WIDTH), BF16),
            jax.ShapeDtypeStruct((B, T, A_WIDTH), BF16),
            jax.ShapeDtypeStruct((B, T, qw), BF16),
            jax.ShapeDtypeStruct((B, T, kw), BF16),
            jax.ShapeDtypeStruct((B, T, kw), BF16),
        ],
        grid=(B, pl.cdiv(T, tm)),
        in_specs=[
            _tok_spec(tm, D),
            _mod_spec(n_lat // tm, B, D),
            _const_spec((1, D)),
            _const_spec(w_in.shape),
            _const_spec((1, A_WIDTH)),
            _const_spec((1, B_HEAD_DIM)),
            _const_spec((1, B_HEAD_DIM)),
            _table_spec(tm), _table_spec(tm), _table_spec(tm),
        ],
        out_specs=[_tok_spec(tm, A_WIDTH), _tok_spec(tm, A_WIDTH), _tok_spec(tm, qw),
                   _tok_spec(tm, kw), _tok_spec(tm, kw)],
        compiler_params=_cparams("arbitrary", "arbitrary"),
        name="ab_in",
    )(X, mod, gpre.reshape(1, D), w_in, a_norm_g.reshape(1, A_WIDTH), qg.reshape(1, -1), kg.reshape(1, -1),
      *tables)
    return outs


def _flash_kernel(*refs, group, dk, dv, kv_len, tk, with_cast, with_zeros):
    refs = list(refs)
    q_ref, k_ref, v_ref = refs[:3]
    w_ref = refs[3] if with_cast else None
    outs = refs[3 + with_cast:]
    o_ref = outs.pop(0)
    if with_cast:
        outs.pop(0)[...] = w_ref[...].astype(BF16)
    if with_zeros:
        z_ref = outs.pop(0)
        z_ref[...] = jnp.zeros(z_ref.shape, z_ref.dtype)
    m_ref, l_ref, acc_ref = outs
    tq = q_ref.shape[1]
    if group == 1:
        q = q_ref[0]
    else:
        q = jnp.concatenate([q_ref[0, :, g * dk:(g + 1) * dk] for g in range(group)], axis=0)
    m_ref[...] = jnp.full(m_ref.shape, -jnp.inf, F32)
    l_ref[...] = jnp.zeros(l_ref.shape, F32)
    acc_ref[...] = jnp.zeros(acc_ref.shape, F32)

    for start in range(0, kv_len, tk):
        size = min(tk, kv_len - start)
        s = _dot_nt(q, k_ref[0, start:start + size, :])
        m_prev = m_ref[...]
        m_new = jnp.maximum(m_prev, jnp.max(s, axis=-1, keepdims=True))
        alpha = jnp.exp2(m_prev - m_new)
        p = [jnp.exp2(s[:, t * LANES:(t + 1) * LANES] - m_new) for t in range(size // LANES)]
        l_ref[...] = alpha * l_ref[...] + functools.reduce(lambda a, b: a + b, p)
        pb = jnp.concatenate([t.astype(BF16) for t in p], axis=1)
        acc_ref[...] = alpha * acc_ref[...] + _dot(pb, v_ref[0, start:start + size, :])
        m_ref[...] = m_new

    out = acc_ref[...] * (1.0 / jnp.sum(l_ref[...], axis=-1, keepdims=True))
    for g in range(group):
        o_ref[0, :, g * dv:(g + 1) * dv] = out[g * tq:(g + 1) * tq].astype(o_ref.dtype)


def _flash(q, k, v, *, n_kv_heads, group, dk, dv, tq, q_row0, n_q_rows, kv_row0, kv_len, name, cast_w=None,
           zeros_shape=None):
    B = q.shape[0]
    q_blk0 = q_row0 // tq
    kv_blk0 = kv_row0 // kv_len
    tk = min(FLASH_KV_CHUNK, kv_len)
    n_q = n_q_rows // tq
    n_steps = B * n_kv_heads * n_q

    def step(b, h, i):
        return (b * n_kv_heads + h) * n_q + i

    in_specs = [
        pl.BlockSpec((1, tq, group * dk), lambda b, h, i: (b, q_blk0 + i, h)),
        pl.BlockSpec((1, kv_len, dk), lambda b, h, i: (b, kv_blk0, h)),
        pl.BlockSpec((1, kv_len, dv), lambda b, h, i: (b, kv_blk0, h)),
    ]
    out_specs = [pl.BlockSpec((1, tq, group * dv), lambda b, h, i: (b, i, h))]
    out_shape = [jax.ShapeDtypeStruct((B, n_q_rows, n_kv_heads * group * dv), BF16)]
    args = [q, k, v]
    if cast_w is not None:
        w_all, li = cast_w
        slabs = w_all.reshape(w_all.shape[0] * n_steps, -1, w_all.shape[-1])
        blk = (1,) + slabs.shape[1:]
        in_specs.append(pl.BlockSpec(blk, lambda b, h, i: (li * n_steps + step(b, h, i), 0, 0)))
        out_specs.append(pl.BlockSpec(blk, lambda b, h, i: (step(b, h, i), 0, 0)))
        out_shape.append(jax.ShapeDtypeStruct((n_steps,) + slabs.shape[1:], BF16))
        args.append(slabs)
    if zeros_shape is not None:
        rows, cols = zeros_shape
        out_specs.append(pl.BlockSpec((rows // n_steps, cols), lambda b, h, i: (step(b, h, i), 0)))
        out_shape.append(jax.ShapeDtypeStruct((rows, cols), F32))
    outs = pl.pallas_call(
        functools.partial(_flash_kernel, group=group, dk=dk, dv=dv, kv_len=kv_len, tk=tk,
                          with_cast=cast_w is not None, with_zeros=zeros_shape is not None),
        out_shape=out_shape,
        grid=(B, n_kv_heads, n_q),
        in_specs=in_specs,
        out_specs=out_specs,
        scratch_shapes=[
            pltpu.VMEM((group * tq, LANES), F32),
            pltpu.VMEM((group * tq, LANES), F32),
            pltpu.VMEM((group * tq, dv), F32),
        ],
        compiler_params=_cparams("arbitrary", "arbitrary", "arbitrary"),
        name=name,
    )(*args)
    outs = list(outs)
    o = outs.pop(0)
    w_b = outs.pop(0).reshape(cast_w[0].shape[1:]) if cast_w is not None else None
    zeros = outs.pop(0) if zeros_shape is not None else None
    return o, w_b, zeros


def _attention(q, k, v, *, n_lat, n_ctx, need_ctx, name, cast_w=None, zeros_shape=None, **kw):
    T = n_lat + n_ctx
    o, w_b, zeros = _flash(q, k, v, q_row0=0, n_q_rows=n_lat, kv_row0=0, kv_len=T, name=name + "_lat",
                           cast_w=cast_w, zeros_shape=zeros_shape, **kw)
    if need_ctx:
        kw = dict(kw, tq=min(kw["tq"], n_ctx))
        o_ctx, _, _ = _flash(q, k, v, q_row0=n_lat, n_q_rows=n_ctx, kv_row0=n_lat, kv_len=n_ctx,
                             name=name + "_ctx", **kw)
        o = jnp.concatenate([o, o_ctx], axis=1)
    return o, w_b, zeros


def _post_mix(y, x, mod, gpost, gpre):
    xn = x + mod[2:3, :] * _rms(y, gpost)
    return xn, _modulated(xn, gpre, mod, 3, 4)


def _ab_out_kernel(u_ref, vg_ref, o_ref, ws_ref, bias_ref, wout_ref, x_ref, mod_ref, gpost_ref, gpre_ref,
                   xo_ref, f_ref):
    tm = u_ref.shape[1]
    y = _dot(o_ref[0], wout_ref[A_WIDTH:, :])
    gated = []
    for c in range(tm // CHUNK):
        rows = slice(c * CHUNK, (c + 1) * CHUNK)
        mixed = jnp.concatenate(
            [_dot(ws_ref[g], vg_ref[0, rows, g * A_GROUP_DIM:(g + 1) * A_GROUP_DIM]) for g in range(A_GROUPS)],
            axis=1) + bias_ref[...]
        gated.append((u_ref[0, rows, :].astype(F32) * mixed).astype(BF16))
    y = y + _dot(jnp.concatenate(gated, axis=0), wout_ref[:A_WIDTH, :])
    xn, f = _post_mix(y, x_ref[0], mod_ref[0], gpost_ref[...], gpre_ref[...])
    xo_ref[0] = xn
    f_ref[0] = f.astype(f_ref.dtype)


def _ab_out(u, vg, o, ws, bias, w_out, X, mod, gpost, gpre, n_lat):
    B, T, D = X.shape
    tm = TOKEN_TILE
    return pl.pallas_call(
        _ab_out_kernel,
        out_shape=[jax.ShapeDtypeStruct((B, T, D), F32), jax.ShapeDtypeStruct((B, T, D), BF16)],
        grid=(B, pl.cdiv(T, tm)),
        in_specs=[
            _tok_spec(tm, A_WIDTH), _tok_spec(tm, A_WIDTH), _tok_spec(tm, o.shape[2]),
            _const_spec(ws.shape), _const_spec(bias.shape), _const_spec(w_out.shape),
            _tok_spec(tm, D), _mod_spec(n_lat // tm, B, D), _const_spec((1, D)), _const_spec((1, D)),
        ],
        out_specs=[_tok_spec(tm, D), _tok_spec(tm, D)],
        compiler_params=_cparams("arbitrary", "arbitrary"),
        name="ab_out",
    )(u, vg, o, ws, bias, w_out, X, mod, gpost.reshape(1, D), gpre.reshape(1, D))


def _ffn_kernel(f_ref, win_ref, wo_ref, x_ref, mod_ref, gpost_ref, xo_ref):
    dff = wo_ref.shape[0]
    fb = f_ref[0]
    acc = None
    for c0 in range(0, dff, FFN_CHUNK):
        gate = _dot(fb, win_ref[:, c0:c0 + FFN_CHUNK])
        up = _dot(fb, win_ref[:, dff + c0:dff + c0 + FFN_CHUNK])
        part = _dot((_silu(gate) * up).astype(BF16), wo_ref[c0:c0 + FFN_CHUNK, :])
        acc = part if acc is None else acc + part
    xo_ref[0] = x_ref[0] + mod_ref[0][5:6, :] * _rms(acc, gpost_ref[...])


def _ffn(f, w_in, w_out, X, mod, gpost, n_lat):
    B, T, D = X.shape
    tm = TOKEN_TILE
    return pl.pallas_call(
        _ffn_kernel,
        out_shape=jax.ShapeDtypeStruct((B, T, D), F32),
        grid=(B, pl.cdiv(T, tm)),
        in_specs=[
            _tok_spec(tm, D), _const_spec(w_in.shape), _const_spec(w_out.shape),
            _tok_spec(tm, D), _mod_spec(n_lat // tm, B, D), _const_spec((1, D)),
        ],
        out_specs=_tok_spec(tm, D),
        compiler_params=_cparams("arbitrary", "arbitrary"),
        name="ffn",
    )(f, w_in, w_out, X, mod, gpost.reshape(1, D))


def _mla_in_kernel(x_ref, mod_ref, gpre_ref, wdq_ref, qng_ref, wuq_ref, wdkv_ref, kvng_ref, wuk_ref, wuv_ref,
                   cos_ref, slo_ref, shi_ref, q_ref, k_ref, v_ref, *, qscale):
    hb = _modulated(x_ref[0], gpre_ref[...], mod_ref[0], 0, 1).astype(BF16)
    cos = cos_ref[...]
    slo = slo_ref[...]
    shi = shi_ref[...]
    quarter = C_ROPE // 4
    hw = 2 * LANES
    cq = _rms(_dot(hb, wdq_ref[...]), qng_ref[...]).astype(BF16)
    qf = _dot(cq, wuq_ref[...])
    for h in range(C_HEADS):
        q_ref[0, :, h * hw:h * hw + LANES] = (qf[:, h * hw:h * hw + LANES] * qscale).astype(BF16)
        qr = _rope(qf[:, h * hw + LANES:(h + 1) * hw], cos, slo, shi, quarter)
        q_ref[0, :, h * hw + LANES:(h + 1) * hw] = (qr * qscale).astype(BF16)
    kvin = _dot(hb, wdkv_ref[...])
    ckv = _rms(kvin[:, :C_KV_RANK], kvng_ref[...]).astype(BF16)
    krope = _rope(kvin[:, C_KV_RANK:], cos, slo, shi, quarter).astype(BF16)
    kn = _dot(ckv, wuk_ref[...])
    for h in range(C_HEADS):
        k_ref[0, :, h * hw:h * hw + LANES] = kn[:, h * C_NOPE:(h + 1) * C_NOPE].astype(BF16)
        k_ref[0, :, h * hw + LANES:(h + 1) * hw] = krope
    v_ref[0] = _dot(ckv, wuv_ref[...]).astype(BF16)


def _mla_in(X, mod, gpre, wdq, qng, wuq, wdkv, kvng, wuk, wuv, tables, n_lat):
    B, T, D = X.shape
    tm = TOKEN_TILE
    qscale = ((C_NOPE + C_ROPE) ** -0.5) * LOG2E
    hw = 2 * LANES
    return pl.pallas_call(
        functools.partial(_mla_in_kernel, qscale=qscale),
        out_shape=[
            jax.ShapeDtypeStruct((B, T, C_HEADS * hw), BF16),
            jax.ShapeDtypeStruct((B, T, C_HEADS * hw), BF16),
            jax.ShapeDtypeStruct((B, T, C_HEADS * C_V), BF16),
        ],
        grid=(B, pl.cdiv(T, tm)),
        in_specs=[
            _tok_spec(tm, D), _mod_spec(n_lat // tm, B, D), _const_spec((1, D)),
            _const_spec(wdq.shape), _const_spec((1, qng.shape[0])), _const_spec(wuq.shape),
            _const_spec(wdkv.shape), _const_spec((1, kvng.shape[0])), _const_spec(wuk.shape),
            _const_spec(wuv.shape),
            _table_spec(tm), _table_spec(tm), _table_spec(tm),
        ],
        out_specs=[_tok_spec(tm, C_HEADS * hw), _tok_spec(tm, C_HEADS * hw), _tok_spec(tm, C_HEADS * C_V)],
        compiler_params=_cparams("arbitrary", "arbitrary"),
        name="mla_in",
    )(X, mod, gpre.reshape(1, D), wdq, qng.reshape(1, -1), wuq, wdkv, kvng.reshape(1, -1), wuk, wuv, *tables)


def _split_bf16(x):
    hi = x.astype(BF16)
    lo = (x - hi.astype(F32)).astype(BF16)
    return hi, lo


def _mla_out_kernel(o_ref, wo_ref, x_ref, mod_ref, gpost_ref, gpre_ref, rt_ref, xo_ref, f_ref, lg_ref):
    y = _dot(o_ref[0], wo_ref[...])
    xn, f = _post_mix(y, x_ref[0], mod_ref[0], gpost_ref[...], gpre_ref[...])
    xo_ref[0] = xn
    f_ref[0] = f
    f_hi, f_lo = _split_bf16(f)
    r_hi, r_lo = _split_bf16(rt_ref[...])
    lg_ref[0] = _dot_nt(r_hi, f_hi) + (_dot_nt(r_hi, f_lo) + _dot_nt(r_lo, f_hi))


def _mla_out(o, w_o, X, mod, gpost, gpre, router_t, n_lat, n_rows):
    B, _, D = X.shape
    tm = TOKEN_TILE
    ne = router_t.shape[0]
    return pl.pallas_call(
        _mla_out_kernel,
        out_shape=[jax.ShapeDtypeStruct((B, n_rows, D), F32), jax.ShapeDtypeStruct((B, n_rows, D), F32),
                   jax.ShapeDtypeStruct((B, ne, n_rows), F32)],
        grid=(B, pl.cdiv(n_rows, tm)),
        in_specs=[
            _tok_spec(tm, D), _const_spec(w_o.shape), _tok_spec(tm, D), _mod_spec(n_lat // tm, B, D),
            _const_spec((1, D)), _const_spec((1, D)), _const_spec(router_t.shape),
        ],
        out_specs=[_tok_spec(tm, D), _tok_spec(tm, D), pl.BlockSpec((1, ne, tm), lambda b, t: (b, 0, t))],
        compiler_params=_cparams("arbitrary", "arbitrary"),
        name="mla_out",
    )(o, w_o, X, mod, gpost.reshape(1, D), gpre.reshape(1, D), router_t)


def _route_kernel(lg_ref, ints_ref, gates_ref, cnt_ref, carry_ref):
    first = jnp.logical_and(pl.program_id(0) == 0, pl.program_id(1) == 0)

    @pl.when(first)
    def _():
        carry_ref[...] = jnp.zeros(carry_ref.shape, F32)

    lg = lg_ref[0]
    ne, tn = lg.shape
    eidx = lax.broadcasted_iota(jnp.int32, (ne, tn), 0)
    m1 = jnp.max(lg, axis=0, keepdims=True)
    i1 = jnp.min(jnp.where(lg == m1, eidx, ne), axis=0, keepdims=True)
    rest = jnp.where(eidx == i1, -jnp.inf, lg)
    m2 = jnp.max(rest, axis=0, keepdims=True)
    i2 = jnp.min(jnp.where(rest == m2, eidx, ne), axis=0, keepdims=True)
    e2 = jnp.exp(m2 - m1)
    den = 1.0 + e2
    w1 = 1.0 / den
    w2 = e2 / den
    member = jnp.logical_or(eidx == i1, eidx == i2)
    before = lax.broadcasted_iota(jnp.int32, (tn, tn), 0) < lax.broadcasted_iota(jnp.int32, (tn, tn), 1)
    cum = _dot(member.astype(BF16), before.astype(BF16)) + carry_ref[:, 0:1]
    r1 = jnp.sum(jnp.where(eidx == i1, cum, 0.0), axis=0, keepdims=True)
    r2 = jnp.sum(jnp.where(eidx == i2, cum, 0.0), axis=0, keepdims=True)
    carry_ref[...] = carry_ref[...] + jnp.sum(member.astype(F32), axis=1, keepdims=True)
    row = lax.broadcasted_iota(jnp.int32, (8, tn), 0)
    ints = jnp.where(row == 0, i1, jnp.where(row == 1, i2, jnp.where(row == 2, r1.astype(jnp.int32),
                                                                      r2.astype(jnp.int32))))
    ints_ref[0] = ints
    gates_ref[0] = jnp.where(row == 0, w1, w2)
    cnt_ref[...] = carry_ref[...]


def _route(logits_t):
    B, ne, n = logits_t.shape
    tn = ROW_DMA_TILE
    return pl.pallas_call(
        _route_kernel,
        out_shape=[jax.ShapeDtypeStruct((B, 8, n), jnp.int32), jax.ShapeDtypeStruct((B, 8, n), F32),
                   jax.ShapeDtypeStruct((ne, LANES), F32)],
        grid=(B, n // tn),
        in_specs=[pl.BlockSpec((1, ne, tn), lambda b, t: (b, 0, t))],
        out_specs=[pl.BlockSpec((1, 8, tn), lambda b, t: (b, 0, t)),
                   pl.BlockSpec((1, 8, tn), lambda b, t: (b, 0, t)),
                   pl.BlockSpec((ne, LANES), lambda b, t: (0, 0))],
        scratch_shapes=[pltpu.VMEM((ne, LANES), F32)],
        compiler_params=_cparams("arbitrary", "arbitrary"),
        name="moe_route",
    )(logits_t)


def _row_copy(src_ref, src_row, dst_ref, dst_row, sem):
    return pltpu.make_async_copy(src_ref.at[pl.ds(src_row, 1), :], dst_ref.at[pl.ds(dst_row, 1), :], sem)


def _dispatch_kernel(pos_ref, f_ref, xs_in_ref, xs_ref, sem):
    del xs_in_ref
    tm = f_ref.shape[1]
    src = f_ref.at[0]

    def issue(r, carry):
        _row_copy(src, r, xs_ref, pos_ref[0, 0, r], sem.at[0]).start(priority=0)
        _row_copy(src, r, xs_ref, pos_ref[0, 1, r], sem.at[1]).start(priority=1)
        return carry

    lax.fori_loop(0, tm, issue, 0, unroll=ROW_DMA_UNROLL)
    for slot in range(2):
        pltpu.make_async_copy(src, xs_ref.at[pl.ds(0, tm), :], sem.at[slot]).wait()


def _dispatch(pos, f, zeros):
    B, n, D = f.shape
    n_sorted = zeros.shape[0]
    tm = ROW_DMA_TILE
    return pl.pallas_call(
        _dispatch_kernel,
        out_shape=jax.ShapeDtypeStruct((n_sorted, D), f.dtype),
        grid=(B, n // tm),
        in_specs=[
            pl.BlockSpec((1, 2, tm), lambda b, t: (b, 0, t), memory_space=pltpu.SMEM),
            _tok_spec(tm, D),
            pl.BlockSpec(memory_space=pl.ANY),
        ],
        out_specs=pl.BlockSpec(memory_space=pl.ANY),
        scratch_shapes=[pltpu.SemaphoreType.DMA((2,))],
        input_output_aliases={2: 0},
        compiler_params=_cparams("arbitrary", "arbitrary"),
        name="moe_dispatch",
    )(pos, f, zeros)


def _expert_kernel(te_ref, ta_ref, xs_ref, wg_ref, wu_ref, wo_ref, y_ref, xb_ref, acc_ref):
    i = pl.program_id(0)
    j = pl.program_id(1)
    last = pl.num_programs(1) - 1
    active = ta_ref[i] == 1

    @pl.when(jnp.logical_and(active, j == 0))
    def _():
        xb_ref[...] = xs_ref[...].astype(BF16)
        acc_ref[...] = jnp.zeros(acc_ref.shape, F32)

    @pl.when(active)
    def _():
        xb = xb_ref[...]
        acc = acc_ref[...]
        for c in range(wg_ref.shape[2] // EXPERT_FF_CHUNK):
            cs = slice(c * EXPERT_FF_CHUNK, (c + 1) * EXPERT_FF_CHUNK)
            h = (_silu(_dot(xb, wg_ref[0, :, cs])) * _dot(xb, wu_ref[0, :, cs])).astype(BF16)
            acc = acc + _dot(h, wo_ref[0, cs, :])
        acc_ref[...] = acc

    @pl.when(jnp.logical_and(active, j == last))
    def _():
        y_ref[...] = acc_ref[...]

    @pl.when(jnp.logical_and(jnp.logical_not(active), j == last))
    def _():
        y_ref[...] = jnp.zeros(y_ref.shape, F32)


def _experts(tile_expert, tile_active, xs, w_in, w_out):
    n_sorted, D = xs.shape
    tm = EXPERT_TILE
    tf = EXPERT_FF_TILE
    dff = w_out.shape[1]
    nf = dff // tf

    def jj(i, j, ta):
        return jnp.where(ta[i] == 1, j, nf - 1)

    grid_spec = pltpu.PrefetchScalarGridSpec(
        num_scalar_prefetch=2,
        grid=(n_sorted // tm, nf),
        in_specs=[
            pl.BlockSpec((tm, D), lambda i, j, te, ta: (i, 0)),
            pl.BlockSpec((1, D, tf), lambda i, j, te, ta: (te[i], 0, jj(i, j, ta))),
            pl.BlockSpec((1, D, tf), lambda i, j, te, ta: (te[i], 0, nf + jj(i, j, ta))),
            pl.BlockSpec((1, tf, D), lambda i, j, te, ta: (te[i], jj(i, j, ta), 0)),
        ],
        out_specs=pl.BlockSpec((tm, D), lambda i, j, te, ta: (i, 0)),
        scratch_shapes=[pltpu.VMEM((tm, D), BF16), pltpu.VMEM((tm, D), F32)],
    )
    return pl.pallas_call(
        _expert_kernel,
        out_shape=jax.ShapeDtypeStruct((n_sorted, D), F32),
        grid_spec=grid_spec,
        compiler_params=_cparams("arbitrary", "arbitrary"),
        name="moe_experts",
    )(tile_expert, tile_active, xs, w_in, w_in, w_out)


def _combine_kernel(pos_ref, ys_ref, g_ref, x_ref, mod_ref, gpost_ref, xo_ref, buf_a, buf_b, sem):
    tm = x_ref.shape[1]

    def issue(r, carry):
        _row_copy(ys_ref, pos_ref[0, 0, r], buf_a, r, sem.at[0]).start(priority=0)
        _row_copy(ys_ref, pos_ref[0, 1, r], buf_b, r, sem.at[1]).start(priority=1)
        return carry

    lax.fori_loop(0, tm, issue, 0, unroll=ROW_DMA_UNROLL)
    pltpu.make_async_copy(ys_ref.at[pl.ds(0, tm), :], buf_a, sem.at[0]).wait()
    pltpu.make_async_copy(ys_ref.at[pl.ds(0, tm), :], buf_b, sem.at[1]).wait()
    g = g_ref[0]
    fo = g[:, 0:1] * buf_a[...] + g[:, 1:2] * buf_b[...]
    xo_ref[0] = x_ref[0] + mod_ref[0][5:6, :] * _rms(fo, gpost_ref[...])


def _combine(pos, ys, gates, X, mod, gpost, n_lat, n_rows):
    B, _, D = X.shape
    tm = ROW_DMA_TILE
    return pl.pallas_call(
        _combine_kernel,
        out_shape=jax.ShapeDtypeStruct((B, n_rows, D), F32),
        grid=(B, n_rows // tm),
        in_specs=[
            pl.BlockSpec((1, 2, tm), lambda b, t: (b, 0, t), memory_space=pltpu.SMEM),
            pl.BlockSpec(memory_space=pl.ANY),
            _tok_spec(tm, 2),
            _tok_spec(tm, D),
            _mod_spec(n_lat // tm, B, D),
            _const_spec((1, D)),
        ],
        out_specs=_tok_spec(tm, D),
        scratch_shapes=[pltpu.VMEM((tm, D), F32), pltpu.VMEM((tm, D), F32), pltpu.SemaphoreType.DMA((2,))],
        compiler_params=_cparams("arbitrary", "arbitrary"),
        name="moe_combine",
    )(pos, ys, gates, X, mod, gpost.reshape(1, D))


def _n_expert_tiles(n_tokens, n_experts):
    return (2 * n_tokens) // EXPERT_TILE + n_experts


def _moe(f, logits_t, w_in, w_out, xs_zeros, X, mod, gpost, n_lat):
    B, n, D = f.shape
    ne = logits_t.shape[1]
    tm = EXPERT_TILE
    ints, gates, counts = _route(logits_t)
    counts = counts[:, 0].astype(jnp.int32)
    padded = ((counts + tm - 1) // tm) * tm
    ends = jnp.cumsum(padded)
    starts = ends - padded
    n_tiles = _n_expert_tiles(B * n, ne)
    def start_of(e):
        return sum(jnp.where(e == j, starts[j], 0) for j in range(ne))

    pos = jnp.stack([start_of(ints[:, 0, :]) + ints[:, 2, :], start_of(ints[:, 1, :]) + ints[:, 3, :]], axis=1)
    tile_start = jnp.arange(n_tiles, dtype=jnp.int32) * tm
    tile_active = (tile_start < ends[-1]).astype(jnp.int32)
    tile_expert = jnp.sum((tile_start[:, None] >= ends[None, :]).astype(jnp.int32), axis=1)
    last_expert = jnp.sum((ends[-1] - 1 >= ends).astype(jnp.int32))
    tile_expert = jnp.where(tile_active == 1, tile_expert, last_expert).astype(jnp.int32)
    xs = _dispatch(pos, f, xs_zeros)
    ys = _experts(tile_expert, tile_active, xs, w_in, w_out)
    g2 = jnp.transpose(gates[:, 0:2, :], (0, 2, 1))
    return _combine(pos, ys, g2, X, mod, gpost, n_lat, n)


def _rope_tables(n_lat, n_ctx, rot_dim):
    rows = n_lat // GRID_W
    n_freq = rot_dim // 4
    inv_freq = ROPE_THETA ** (-jnp.arange(n_freq, dtype=F32) / n_freq)
    row_ang = jnp.arange(rows, dtype=F32)[:, None] * inv_freq
    col_ang = jnp.arange(GRID_W, dtype=F32)[:, None] * inv_freq

    def per_position(fn):
        r = fn(row_ang)
        c = fn(col_ang)
        return jnp.concatenate([jnp.repeat(jnp.concatenate([r, r], axis=1), GRID_W, axis=0),
                                jnp.tile(jnp.concatenate([c, c], axis=1), (rows, 1))], axis=1)

    cos = per_position(jnp.cos)
    sin = per_position(jnp.sin)
    first = (jnp.arange(rot_dim) % (2 * n_freq)) < n_freq
    sin_lo = jnp.where(first[None, :], -sin, 0.0)
    sin_hi = jnp.where(first[None, :], 0.0, sin)

    def finish(t, ctx_value):
        t = jnp.concatenate([t, jnp.full((n_ctx, rot_dim), ctx_value, F32)], axis=0)
        return jnp.pad(t, ((0, 0), (0, LANES - rot_dim)))

    return finish(cos, 1.0), finish(sin_lo, 0.0), finish(sin_hi, 0.0)


def kernel(x, c, ctx, c_ctx, ada_w, ada_b, g_pre_mix, g_post_mix, g_pre_ffn, g_post_ffn, ab_w_in, a_norm_g, a_ws,
           a_bs, b_qnorm_g, b_knorm_g, ab_w_out, ffn_w_in, ffn_w_out, mla_w_dq, mla_qnorm_g, mla_w_uq, mla_w_dkv,
           mla_kvnorm_g, mla_w_ukv, mla_w_o, moe_router, moe_w_in, moe_w_out):
    B, S, D = x.shape
    NC = ctx.shape[1]
    depth = ada_w.shape[0]
    assert S % TOKEN_TILE == 0 and S % GRID_W == 0 and NC % ROW_DMA_TILE == 0 and TOKEN_TILE % NC == 0

    X = jnp.concatenate([x, ctx], axis=1)
    cpad = jnp.zeros((8, D), F32).at[:B].set(c).at[B].set(c_ctx)
    mods = _adaln(cpad, ada_w, ada_b)
    tables_b = _rope_tables(S, NC, B_HEAD_DIM)
    tables_c = _rope_tables(S, NC, C_ROPE)
    moe_w_out_b = None

    for layer in range(depth):
        need_ctx = layer < depth - 1
        i = layer // 2
        mod = mods[layer]
        n_rows = S + NC if need_ctx else S
        if layer % 2 == 0:
            u, vg, q, k, v = _ab_in(X, mod, g_pre_mix[layer], ab_w_in[i].astype(BF16), a_norm_g[i], b_qnorm_g[i],
                                    b_knorm_g[i], tables_b, S)
            cast = (moe_w_out, i) if layer + 1 < depth else None
            o, moe_w_out_b, _ = _attention(q, k, v, n_lat=S, n_ctx=NC, need_ctx=True, name="gqa",
                                        n_kv_heads=B_KV_HEADS, group=B_HEADS // B_KV_HEADS, dk=B_HEAD_DIM,
                                        dv=B_HEAD_DIM, tq=512, cast_w=cast)
            bias = jnp.broadcast_to(a_bs[i].T[:, :, None], (CHUNK, A_GROUPS, A_GROUP_DIM)).reshape(CHUNK, A_WIDTH)
            X, f = _ab_out(u, vg, o, a_ws[i].astype(BF16), bias, ab_w_out[i].astype(BF16), X, mod,
                           g_post_mix[layer], g_pre_ffn[layer], S)
            X = _ffn(f, ffn_w_in[i].astype(BF16), ffn_w_out[i].astype(BF16), X, mod, g_post_ffn[layer], S)
        else:
            hw = 2 * LANES
            wuq = mla_w_uq[i].reshape(-1, C_HEADS, C_NOPE + C_ROPE)
            wuq = jnp.pad(wuq, ((0, 0), (0, 0), (0, hw - C_NOPE - C_ROPE))).reshape(-1, C_HEADS * hw)
            wdkv = jnp.pad(mla_w_dkv[i], ((0, 0), (0, LANES - C_ROPE)))
            wukv = mla_w_ukv[i].reshape(C_KV_RANK, C_HEADS, C_NOPE + C_V)
            wuk = wukv[:, :, :C_NOPE].reshape(C_KV_RANK, C_HEADS * C_NOPE)
            wuv = wukv[:, :, C_NOPE:].reshape(C_KV_RANK, C_HEADS * C_V)
            q, k, v = _mla_in(X, mod, g_pre_mix[layer], mla_w_dq[i].astype(BF16), mla_qnorm_g[i],
                              wuq.astype(BF16), wdkv.astype(BF16), mla_kvnorm_g[i], wuk.astype(BF16),
                              wuv.astype(BF16), tables_c, S)
            n_sorted = _n_expert_tiles(B * n_rows, moe_router.shape[-1]) * EXPERT_TILE
            o, moe_w_in_b, xs_zeros = _attention(q, k, v, n_lat=S, n_ctx=NC, need_ctx=need_ctx, name="mla",
                                                 n_kv_heads=C_HEADS, group=1, dk=hw, dv=C_V, tq=1024,
                                                 cast_w=(moe_w_in, i), zeros_shape=(n_sorted, D))
            X, f, logits_t = _mla_out(o, mla_w_o[i].astype(BF16), X, mod, g_post_mix[layer], g_pre_ffn[layer],
                                      moe_router[i].T, S, n_rows)
            X = _moe(f, logits_t, moe_w_in_b, moe_w_out_b, xs_zeros, X, mod, g_post_ffn[layer], S)
    return X[:, :S] if X.shape[1] != S else X
```

```python
import functools
import math

import jax
import jax.numpy as jnp
from jax import lax
from jax.experimental import pallas as pl
from jax.experimental.pallas import tpu as pltpu

F32 = jnp.float32
BF16 = jnp.bfloat16

GRID_W = 64
ROPE_THETA = 10000.0
EPS = 1e-6
CHUNK = 128
A_GROUPS = 4
A_GROUP_DIM = 128
A_WIDTH = A_GROUPS * A_GROUP_DIM
B_HEADS = 4
B_KV_HEADS = 2
B_HEAD_DIM = 128
C_HEADS = 8
C_KV_RANK = 256
C_NOPE = 128
C_ROPE = 64
C_V = 128
N_EXPERTS = 8
LANES = 128

VMEM_LIMIT_BYTES = 56 * 1024 * 1024
TOKEN_TILE = 512
TOKEN_SUBTILE = 256
ROW_DMA_TILE = 256
FLASH_KV_CHUNK = 2816
EXPERT_TILE = 512
EXPERT_FF_TILE = 1792
EXPERT_FF_CHUNK = 256
ROW_DMA_UNROLL = 8
FFN_CHUNK = 256
LOG2E = math.log2(math.e)


def _cparams(*sem):
    return pltpu.CompilerParams(dimension_semantics=sem, vmem_limit_bytes=VMEM_LIMIT_BYTES)


def _dot(a, b):
    return jnp.dot(a, b, preferred_element_type=F32)


def _dot_nt(a, b):
    return lax.dot_general(a, b, (((1,), (1,)), ((), ())), preferred_element_type=F32)


def _rms(x, g):
    return x * lax.rsqrt(jnp.mean(x * x, axis=-1, keepdims=True) + EPS) * g


def _gelu(x):
    c = math.sqrt(2.0 / math.pi)
    return x * (0.5 * (1.0 + jnp.tanh(c * (x + 0.044715 * (x * x * x)))))


def _silu(x):
    return x * (1.0 / (1.0 + jnp.exp(-x)))


def _rope(x, cos, sin_lo, sin_hi, quarter):
    return (x * cos + pltpu.roll(x, LANES - quarter, 1) * sin_lo
            + pltpu.roll(x, quarter, 1) * sin_hi)


def _modulated(x, g, mod, shift_row, scale_row):
    return (_rms(x, g) * (1.0 + mod[scale_row:scale_row + 1, :])
            + mod[shift_row:shift_row + 1, :])


def _adaln_kernel(c_ref, w_ref, b_ref, o_ref):
    act = _silu(c_ref[...]).astype(BF16)
    o_ref[0, 0] = _dot(act, w_ref[0].astype(BF16)) + b_ref[0, 0]


def _adaln(cpad, ada_w, ada_b):
    depth, d, _ = ada_w.shape
    rows = cpad.shape[0]
    out = pl.pallas_call(
        _adaln_kernel,
        out_shape=jax.ShapeDtypeStruct((depth, 6, rows, d), F32),
        grid=(depth, 6),
        in_specs=[
            pl.BlockSpec((rows, d), lambda l, j: (0, 0)),
            pl.BlockSpec((1, d, d), lambda l, j: (l, 0, j)),
            pl.BlockSpec((1, 1, 1, d), lambda l, j: (l, j, 0, 0)),
        ],
        out_specs=pl.BlockSpec((1, 1, rows, d), lambda l, j: (l, j, 0, 0)),
        compiler_params=_cparams("arbitrary", "arbitrary"),
        name="adaln",
    )(cpad, ada_w, ada_b.reshape(depth, 6, 1, d))
    return jnp.transpose(out, (0, 2, 1, 3))


def _tok_spec(tm, width):
    return pl.BlockSpec((1, tm, width), lambda b, t: (b, t, 0))


def _mod_spec(n_lat_tiles, ctx_row, d):
    return pl.BlockSpec((1, 6, d), lambda b, t: (jnp.where(t >= n_lat_tiles, ctx_row, b), 0, 0))


def _const_spec(shape):
    nd = len(shape)
    return pl.BlockSpec(shape, lambda b, t: (0,) * nd)


def _table_spec(tm):
    return pl.BlockSpec((tm, LANES), lambda b, t: (t, 0))


def _ab_in_kernel(*refs, qscale, n_lat_tiles, from_parts):
    if from_parts:
        (x_ref, ctx_ref, mod_ref, gpre_ref, w_ref, ang_ref, qg_ref, kg_ref, cos_ref, slo_ref, shi_ref,
         u_ref, vg_ref, q_ref, k_ref, v_ref, xo_ref) = refs
        reps = x_ref.shape[1] // ctx_ref.shape[1]
        ctx_tile = jnp.concatenate([ctx_ref[0]] * reps, axis=0)
        xo_ref[0] = jnp.where(pl.program_id(1) >= n_lat_tiles, ctx_tile, x_ref[0])
        x_ref = xo_ref
    else:
        (x_ref, mod_ref, gpre_ref, w_ref, ang_ref, qg_ref, kg_ref, cos_ref, slo_ref, shi_ref,
         u_ref, vg_ref, q_ref, k_ref, v_ref) = refs
    quarter = B_HEAD_DIM // 4
    c0 = 2 * A_WIDTH
    c1 = c0 + B_HEADS * B_HEAD_DIM
    c2 = c1 + B_KV_HEADS * B_HEAD_DIM
    tm = x_ref.shape[1]
    for r0 in range(0, tm, TOKEN_SUBTILE):
        rows = slice(r0, r0 + TOKEN_SUBTILE)
        hb = _modulated(x_ref[0, rows, :], gpre_ref[...], mod_ref[0], 0, 1).astype(BF16)
        cos = cos_ref[rows, :]
        slo = slo_ref[rows, :]
        shi = shi_ref[rows, :]
        u_ref[0, rows, :] = _gelu(_dot(hb, w_ref[:, 0:A_WIDTH])).astype(BF16)
        v = _gelu(_dot(hb, w_ref[:, A_WIDTH:c0]))
        for g in range(A_GROUPS):
            sl = slice(g * A_GROUP_DIM, (g + 1) * A_GROUP_DIM)
            vg_ref[0, rows, sl] = _rms(v[:, sl], ang_ref[:, sl]).astype(BF16)
        q = _dot(hb, w_ref[:, c0:c1])
        for h in range(B_HEADS):
            sl = slice(h * B_HEAD_DIM, (h + 1) * B_HEAD_DIM)
            qr = _rope(_rms(q[:, sl], qg_ref[...]), cos, slo, shi, quarter)
            q_ref[0, rows, sl] = (qr * qscale).astype(BF16)
        k = _dot(hb, w_ref[:, c1:c2])
        for h in range(B_KV_HEADS):
            sl = slice(h * B_HEAD_DIM, (h + 1) * B_HEAD_DIM)
            k_ref[0, rows, sl] = _rope(_rms(k[:, sl], kg_ref[...]), cos, slo, shi, quarter).astype(BF16)
        v_ref[0, rows, :] = _dot(hb, w_ref[:, c2:c2 + B_KV_HEADS * B_HEAD_DIM]).astype(BF16)


def _ab_in(X, mod, gpre, w_in, a_norm_g, qg, kg, tables, n_lat, parts=None):
    tm = TOKEN_TILE
    n_lat_tiles = n_lat // tm
    if parts is None:
        B, T, D = X.shape
        srcs = [X]
        src_specs = [_tok_spec(tm, D)]
    else:
        lat, ctx_in = parts
        B, _, D = lat.shape
        T = n_lat + ctx_in.shape[1]
        srcs = [lat, ctx_in]
        src_specs = [pl.BlockSpec((1, tm, D), lambda b, t: (b, jnp.minimum(t, n_lat_tiles - 1), 0)),
                     pl.BlockSpec((1, ctx_in.shape[1], D), lambda b, t: (b, 0, 0))]
    qw = B_HEADS * B_HEAD_DIM
    kw = B_KV_HEADS * B_HEAD_DIM
    qscale = (B_HEAD_DIM ** -0.5) * LOG2E
    outs = pl.pallas_call(
        functools.partial(_ab_in_kernel, qscale=qscale, n_lat_tiles=n_lat_tiles, from_parts=parts is not None),
        out_shape=[
            jax.ShapeDtypeStruct((B, T, A_WIDTH), BF16),
            jax.ShapeDtypeStruct((B, T, A_WIDTH), BF16),
            jax.ShapeDtypeStruct((B, T, qw), BF16),
            jax.ShapeDtypeStruct((B, T, kw), BF16),
            jax.ShapeDtypeStruct((B, T, kw), BF16),
        ] + ([jax.ShapeDtypeStruct((B, T, D), F32)] if parts is not None else []),
        grid=(B, pl.cdiv(T, tm)),
        in_specs=src_specs + [
            _mod_spec(n_lat_tiles, B, D),
            _const_spec((1, D)),
            _const_spec(w_in.shape),
            _const_spec((1, A_WIDTH)),
            _const_spec((1, B_HEAD_DIM)),
            _const_spec((1, B_HEAD_DIM)),
            _table_spec(tm), _table_spec(tm), _table_spec(tm),
        ],
        out_specs=[_tok_spec(tm, A_WIDTH), _tok_spec(tm, A_WIDTH), _tok_spec(tm, qw),
                   _tok_spec(tm, kw), _tok_spec(tm, kw)] + ([_tok_spec(tm, D)] if parts is not None else []),
        compiler_params=_cparams("arbitrary", "arbitrary"),
        name="ab_in",
    )(*srcs, mod, gpre.reshape(1, D), w_in, a_norm_g.reshape(1, A_WIDTH), qg.reshape(1, -1), kg.reshape(1, -1),
      *tables)
    return outs


def _flash_kernel(*refs, group, dk, dv, kv_len, tk, with_cast, with_zeros):
    refs = list(refs)
    q_ref, k_ref, v_ref = refs[:3]
    w_ref = refs[3] if with_cast else None
    outs = refs[3 + with_cast:]
    o_ref = outs.pop(0)
    if with_cast:
        outs.pop(0)[...] = w_ref[...].astype(BF16)
    if with_zeros:
        z_ref = outs.pop(0)
        z_ref[...] = jnp.zeros(z_ref.shape, z_ref.dtype)
    m_ref, l_ref, acc_ref = outs
    tq = q_ref.shape[1]
    if group == 1:
        q = q_ref[0]
    else:
        q = jnp.concatenate([q_ref[0, :, g * dk:(g + 1) * dk] for g in range(group)], axis=0)
    m_ref[...] = jnp.full(m_ref.shape, -jnp.inf, F32)
    l_ref[...] = jnp.zeros(l_ref.shape, F32)
    acc_ref[...] = jnp.zeros(acc_ref.shape, F32)

    for start in range(0, kv_len, tk):
        size = min(tk, kv_len - start)
        s = _dot_nt(q, k_ref[0, start:start + size, :])
        m_prev = m_ref[...]
        m_new = jnp.maximum(m_prev, jnp.max(s, axis=-1, keepdims=True))
        alpha = jnp.exp2(m_prev - m_new)
        p = [jnp.exp2(s[:, t * LANES:(t + 1) * LANES] - m_new) for t in range(size // LANES)]
        l_ref[...] = alpha * l_ref[...] + functools.reduce(lambda a, b: a + b, p)
        pb = jnp.concatenate([t.astype(BF16) for t in p], axis=1)
        acc_ref[...] = alpha * acc_ref[...] + _dot(pb, v_ref[0, start:start + size, :])
        m_ref[...] = m_new

    out = acc_ref[...] * (1.0 / jnp.sum(l_ref[...], axis=-1, keepdims=True))
    for g in range(group):
        o_ref[0, :, g * dv:(g + 1) * dv] = out[g * tq:(g + 1) * tq].astype(o_ref.dtype)


def _flash(q, k, v, *, n_kv_heads, group, dk, dv, tq, q_row0, n_q_rows, kv_row0, kv_len, name, cast_w=None,
           zeros_shape=None):
    B = q.shape[0]
    q_blk0 = q_row0 // tq
    kv_blk0 = kv_row0 // kv_len
    tk = min(FLASH_KV_CHUNK, kv_len)
    n_q = n_q_rows // tq
    n_steps = B * n_kv_heads * n_q

    def step(b, h, i):
        return (b * n_kv_heads + h) * n_q + i

    in_specs = [
        pl.BlockSpec((1, tq, group * dk), lambda b, h, i: (b, q_blk0 + i, h)),
        pl.BlockSpec((1, kv_len, dk), lambda b, h, i: (b, kv_blk0, h)),
        pl.BlockSpec((1, kv_len, dv), lambda b, h, i: (b, kv_blk0, h)),
    ]
    out_specs = [pl.BlockSpec((1, tq, group * dv), lambda b, h, i: (b, i, h))]
    out_shape = [jax.ShapeDtypeStruct((B, n_q_rows, n_kv_heads * group * dv), BF16)]
    args = [q, k, v]
    if cast_w is not None:
        w_all, li = cast_w
        slabs = w_all.reshape(w_all.shape[0] * n_steps, -1, w_all.shape[-1])
        blk = (1,) + slabs.shape[1:]
        in_specs.append(pl.BlockSpec(blk, lambda b, h, i: (li * n_steps + step(b, h, i), 0, 0)))
        out_specs.append(pl.BlockSpec(blk, lambda b, h, i: (step(b, h, i), 0, 0)))
        out_shape.append(jax.ShapeDtypeStruct((n_steps,) + slabs.shape[1:], BF16))
        args.append(slabs)
    if zeros_shape is not None:
        rows, cols = zeros_shape
        out_specs.append(pl.BlockSpec((rows // n_steps, cols), lambda b, h, i: (step(b, h, i), 0)))
        out_shape.append(jax.ShapeDtypeStruct((rows, cols), F32))
    outs = pl.pallas_call(
        functools.partial(_flash_kernel, group=group, dk=dk, dv=dv, kv_len=kv_len, tk=tk,
                          with_cast=cast_w is not None, with_zeros=zeros_shape is not None),
        out_shape=out_shape,
        grid=(B, n_kv_heads, n_q),
        in_specs=in_specs,
        out_specs=out_specs,
        scratch_shapes=[
            pltpu.VMEM((group * tq, LANES), F32),
            pltpu.VMEM((group * tq, LANES), F32),
            pltpu.VMEM((group * tq, dv), F32),
        ],
        compiler_params=_cparams("arbitrary", "arbitrary", "arbitrary"),
        name=name,
    )(*args)
    outs = list(outs)
    o = outs.pop(0)
    w_b = outs.pop(0).reshape(cast_w[0].shape[1:]) if cast_w is not None else None
    zeros = outs.pop(0) if zeros_shape is not None else None
    return o, w_b, zeros


def _attention(q, k, v, *, n_lat, n_ctx, need_ctx, name, cast_w=None, zeros_shape=None, **kw):
    T = n_lat + n_ctx
    o, w_b, zeros = _flash(q, k, v, q_row0=0, n_q_rows=n_lat, kv_row0=0, kv_len=T, name=name + "_lat",
                           cast_w=cast_w, zeros_shape=zeros_shape, **kw)
    if need_ctx:
        kw = dict(kw, tq=min(kw["tq"], n_ctx))
        o_ctx, _, _ = _flash(q, k, v, q_row0=n_lat, n_q_rows=n_ctx, kv_row0=n_lat, kv_len=n_ctx,
                             name=name + "_ctx", **kw)
        o = jnp.concatenate([o, o_ctx], axis=1)
    return o, w_b, zeros


def _post_mix(y, x, mod, gpost, gpre):
    xn = x + mod[2:3, :] * _rms(y, gpost)
    return xn, _modulated(xn, gpre, mod, 3, 4)


def _ab_out_kernel(u_ref, vg_ref, o_ref, ws_ref, bias_ref, wout_ref, x_ref, mod_ref, gpost_ref, gpre_ref,
                   xo_ref, f_ref):
    tm = u_ref.shape[1]
    y = _dot(o_ref[0], wout_ref[A_WIDTH:, :])
    gated = []
    for c in range(tm // CHUNK):
        rows = slice(c * CHUNK, (c + 1) * CHUNK)
        mixed = jnp.concatenate(
            [_dot(ws_ref[g], vg_ref[0, rows, g * A_GROUP_DIM:(g + 1) * A_GROUP_DIM]) for g in range(A_GROUPS)],
            axis=1) + bias_ref[...]
        gated.append((u_ref[0, rows, :].astype(F32) * mixed).astype(BF16))
    y = y + _dot(jnp.concatenate(gated, axis=0), wout_ref[:A_WIDTH, :])
    xn, f = _post_mix(y, x_ref[0], mod_ref[0], gpost_ref[...], gpre_ref[...])
    xo_ref[0] = xn
    f_ref[0] = f.astype(f_ref.dtype)


def _ab_out(u, vg, o, ws, bias, w_out, X, mod, gpost, gpre, n_lat):
    B, T, D = X.shape
    tm = TOKEN_TILE
    return pl.pallas_call(
        _ab_out_kernel,
        out_shape=[jax.ShapeDtypeStruct((B, T, D), F32), jax.ShapeDtypeStruct((B, T, D), BF16)],
        grid=(B, pl.cdiv(T, tm)),
        in_specs=[
            _tok_spec(tm, A_WIDTH), _tok_spec(tm, A_WIDTH), _tok_spec(tm, o.shape[2]),
            _const_spec(ws.shape), _const_spec(bias.shape), _const_spec(w_out.shape),
            _tok_spec(tm, D), _mod_spec(n_lat // tm, B, D), _const_spec((1, D)), _const_spec((1, D)),
        ],
        out_specs=[_tok_spec(tm, D), _tok_spec(tm, D)],
        compiler_params=_cparams("arbitrary", "arbitrary"),
        name="ab_out",
    )(u, vg, o, ws, bias, w_out, X, mod, gpost.reshape(1, D), gpre.reshape(1, D))


def _ffn_kernel(f_ref, win_ref, wo_ref, x_ref, mod_ref, gpost_ref, xo_ref):
    dff = wo_ref.shape[0]
    fb = f_ref[0]
    acc = None
    for c0 in range(0, dff, FFN_CHUNK):
        gate = _dot(fb, win_ref[:, c0:c0 + FFN_CHUNK])
        up = _dot(fb, win_ref[:, dff + c0:dff + c0 + FFN_CHUNK])
        part = _dot((_silu(gate) * up).astype(BF16), wo_ref[c0:c0 + FFN_CHUNK, :])
        acc = part if acc is None else acc + part
    xo_ref[0] = x_ref[0] + mod_ref[0][5:6, :] * _rms(acc, gpost_ref[...])


def _ffn(f, w_in, w_out, X, mod, gpost, n_lat):
    B, T, D = X.shape
    tm = TOKEN_TILE
    return pl.pallas_call(
        _ffn_kernel,
        out_shape=jax.ShapeDtypeStruct((B, T, D), F32),
        grid=(B, pl.cdiv(T, tm)),
        in_specs=[
            _tok_spec(tm, D), _const_spec(w_in.shape), _const_spec(w_out.shape),
            _tok_spec(tm, D), _mod_spec(n_lat // tm, B, D), _const_spec((1, D)),
        ],
        out_specs=_tok_spec(tm, D),
        compiler_params=_cparams("arbitrary", "arbitrary"),
        name="ffn",
    )(f, w_in, w_out, X, mod, gpost.reshape(1, D))


def _mla_in_kernel(x_ref, mod_ref, gpre_ref, wdq_ref, qng_ref, wuq_ref, wdkv_ref, kvng_ref, wuk_ref, wuv_ref,
                   cos_ref, slo_ref, shi_ref, q_ref, k_ref, v_ref, *, qscale):
    hb = _modulated(x_ref[0], gpre_ref[...], mod_ref[0], 0, 1).astype(BF16)
    cos = cos_ref[...]
    slo = slo_ref[...]
    shi = shi_ref[...]
    quarter = C_ROPE // 4
    hw = 2 * LANES
    cq = _rms(_dot(hb, wdq_ref[...]), qng_ref[...]).astype(BF16)
    qf = _dot(cq, wuq_ref[...])
    for h in range(C_HEADS):
        q_ref[0, :, h * hw:h * hw + LANES] = (qf[:, h * hw:h * hw + LANES] * qscale).astype(BF16)
        qr = _rope(qf[:, h * hw + LANES:(h + 1) * hw], cos, slo, shi, quarter)
        q_ref[0, :, h * hw + LANES:(h + 1) * hw] = (qr * qscale).astype(BF16)
    kvin = _dot(hb, wdkv_ref[...])
    ckv = _rms(kvin[:, :C_KV_RANK], kvng_ref[...]).astype(BF16)
    krope = _rope(kvin[:, C_KV_RANK:], cos, slo, shi, quarter).astype(BF16)
    kn = _dot(ckv, wuk_ref[...])
    for h in range(C_HEADS):
        k_ref[0, :, h * hw:h * hw + LANES] = kn[:, h * C_NOPE:(h + 1) * C_NOPE].astype(BF16)
        k_ref[0, :, h * hw + LANES:(h + 1) * hw] = krope
    v_ref[0] = _dot(ckv, wuv_ref[...]).astype(BF16)


def _mla_in(X, mod, gpre, wdq, qng, wuq, wdkv, kvng, wuk, wuv, tables, n_lat):
    B, T, D = X.shape
    tm = TOKEN_TILE
    qscale = ((C_NOPE + C_ROPE) ** -0.5) * LOG2E
    hw = 2 * LANES
    return pl.pallas_call(
        functools.partial(_mla_in_kernel, qscale=qscale),
        out_shape=[
            jax.ShapeDtypeStruct((B, T, C_HEADS * hw), BF16),
            jax.ShapeDtypeStruct((B, T, C_HEADS * hw), BF16),
            jax.ShapeDtypeStruct((B, T, C_HEADS * C_V), BF16),
        ],
        grid=(B, pl.cdiv(T, tm)),
        in_specs=[
            _tok_spec(tm, D), _mod_spec(n_lat // tm, B, D), _const_spec((1, D)),
            _const_spec(wdq.shape), _const_spec((1, qng.shape[0])), _const_spec(wuq.shape),
            _const_spec(wdkv.shape), _const_spec((1, kvng.shape[0])), _const_spec(wuk.shape),
            _const_spec(wuv.shape),
            _table_spec(tm), _table_spec(tm), _table_spec(tm),
        ],
        out_specs=[_tok_spec(tm, C_HEADS * hw), _tok_spec(tm, C_HEADS * hw), _tok_spec(tm, C_HEADS * C_V)],
        compiler_params=_cparams("arbitrary", "arbitrary"),
        name="mla_in",
    )(X, mod, gpre.reshape(1, D), wdq, qng.reshape(1, -1), wuq, wdkv, kvng.reshape(1, -1), wuk, wuv, *tables)


def _split_bf16(x):
    hi = x.astype(BF16)
    lo = (x - hi.astype(F32)).astype(BF16)
    return hi, lo


def _mla_out_kernel(o_ref, wo_ref, x_ref, mod_ref, gpost_ref, gpre_ref, rt_ref, xo_ref, f_ref, lg_ref):
    y = _dot(o_ref[0], wo_ref[...])
    xn, f = _post_mix(y, x_ref[0], mod_ref[0], gpost_ref[...], gpre_ref[...])
    xo_ref[0] = xn
    f_ref[0] = f
    f_hi, f_lo = _split_bf16(f)
    r_hi, r_lo = _split_bf16(rt_ref[...])
    lg_ref[0] = _dot_nt(r_hi, f_hi) + (_dot_nt(r_hi, f_lo) + _dot_nt(r_lo, f_hi))


def _mla_out(o, w_o, X, mod, gpost, gpre, router_t, n_lat, n_rows):
    B, _, D = X.shape
    tm = TOKEN_TILE
    ne = router_t.shape[0]
    return pl.pallas_call(
        _mla_out_kernel,
        out_shape=[jax.ShapeDtypeStruct((B, n_rows, D), F32), jax.ShapeDtypeStruct((B, n_rows, D), F32),
                   jax.ShapeDtypeStruct((B, ne, n_rows), F32)],
        grid=(B, pl.cdiv(n_rows, tm)),
        in_specs=[
            _tok_spec(tm, D), _const_spec(w_o.shape), _tok_spec(tm, D), _mod_spec(n_lat // tm, B, D),
            _const_spec((1, D)), _const_spec((1, D)), _const_spec(router_t.shape),
        ],
        out_specs=[_tok_spec(tm, D), _tok_spec(tm, D), pl.BlockSpec((1, ne, tm), lambda b, t: (b, 0, t))],
        compiler_params=_cparams("arbitrary", "arbitrary"),
        name="mla_out",
    )(o, w_o, X, mod, gpost.reshape(1, D), gpre.reshape(1, D), router_t)


def _route_kernel(lg_ref, ints_ref, gates_ref, cnt_ref, carry_ref):
    first = jnp.logical_and(pl.program_id(0) == 0, pl.program_id(1) == 0)

    @pl.when(first)
    def _():
        carry_ref[...] = jnp.zeros(carry_ref.shape, F32)

    lg = lg_ref[0]
    ne, tn = lg.shape
    eidx = lax.broadcasted_iota(jnp.int32, (ne, tn), 0)
    m1 = jnp.max(lg, axis=0, keepdims=True)
    i1 = jnp.min(jnp.where(lg == m1, eidx, ne), axis=0, keepdims=True)
    rest = jnp.where(eidx == i1, -jnp.inf, lg)
    m2 = jnp.max(rest, axis=0, keepdims=True)
    i2 = jnp.min(jnp.where(rest == m2, eidx, ne), axis=0, keepdims=True)
    e2 = jnp.exp(m2 - m1)
    den = 1.0 + e2
    w1 = 1.0 / den
    w2 = e2 / den
    member = jnp.logical_or(eidx == i1, eidx == i2)
    before = lax.broadcasted_iota(jnp.int32, (tn, tn), 0) < lax.broadcasted_iota(jnp.int32, (tn, tn), 1)
    cum = _dot(member.astype(BF16), before.astype(BF16)) + carry_ref[:, 0:1]
    r1 = jnp.sum(jnp.where(eidx == i1, cum, 0.0), axis=0, keepdims=True)
    r2 = jnp.sum(jnp.where(eidx == i2, cum, 0.0), axis=0, keepdims=True)
    carry_ref[...] = carry_ref[...] + jnp.sum(member.astype(F32), axis=1, keepdims=True)
    row = lax.broadcasted_iota(jnp.int32, (8, tn), 0)
    ints = jnp.where(row == 0, i1, jnp.where(row == 1, i2, jnp.where(row == 2, r1.astype(jnp.int32),
                                                                      r2.astype(jnp.int32))))
    ints_ref[0] = ints
    gates_ref[0] = jnp.where(row == 0, w1, w2)
    cnt_ref[...] = carry_ref[...]


def _route(logits_t):
    B, ne, n = logits_t.shape
    tn = ROW_DMA_TILE
    return pl.pallas_call(
        _route_kernel,
        out_shape=[jax.ShapeDtypeStruct((B, 8, n), jnp.int32), jax.ShapeDtypeStruct((B, 8, n), F32),
                   jax.ShapeDtypeStruct((ne, LANES), F32)],
        grid=(B, n // tn),
        in_specs=[pl.BlockSpec((1, ne, tn), lambda b, t: (b, 0, t))],
        out_specs=[pl.BlockSpec((1, 8, tn), lambda b, t: (b, 0, t)),
                   pl.BlockSpec((1, 8, tn), lambda b, t: (b, 0, t)),
                   pl.BlockSpec((ne, LANES), lambda b, t: (0, 0))],
        scratch_shapes=[pltpu.VMEM((ne, LANES), F32)],
        compiler_params=_cparams("arbitrary", "arbitrary"),
        name="moe_route",
    )(logits_t)


def _row_copy(src_ref, src_row, dst_ref, dst_row, sem):
    return pltpu.make_async_copy(src_ref.at[pl.ds(src_row, 1), :], dst_ref.at[pl.ds(dst_row, 1), :], sem)


def _dispatch_kernel(pos_ref, f_ref, xs_in_ref, xs_ref, sem):
    del xs_in_ref
    tm = f_ref.shape[1]
    src = f_ref.at[0]

    def issue(r, carry):
        _row_copy(src, r, xs_ref, pos_ref[0, 0, r], sem.at[0]).start()
        _row_copy(src, r, xs_ref, pos_ref[0, 1, r], sem.at[1]).start()
        return carry

    lax.fori_loop(0, tm, issue, 0, unroll=ROW_DMA_UNROLL)
    for slot in range(2):
        pltpu.make_async_copy(src, xs_ref.at[pl.ds(0, tm), :], sem.at[slot]).wait()


def _dispatch(pos, f, zeros):
    B, n, D = f.shape
    n_sorted = zeros.shape[0]
    tm = ROW_DMA_TILE
    return pl.pallas_call(
        _dispatch_kernel,
        out_shape=jax.ShapeDtypeStruct((n_sorted, D), f.dtype),
        grid=(B, n // tm),
        in_specs=[
            pl.BlockSpec((1, 2, tm), lambda b, t: (b, 0, t), memory_space=pltpu.SMEM),
            _tok_spec(tm, D),
            pl.BlockSpec(memory_space=pl.ANY),
        ],
        out_specs=pl.BlockSpec(memory_space=pl.ANY),
        scratch_shapes=[pltpu.SemaphoreType.DMA((2,))],
        input_output_aliases={2: 0},
        compiler_params=_cparams("arbitrary", "arbitrary"),
        name="moe_dispatch",
    )(pos, f, zeros)


def _expert_kernel(te_ref, ta_ref, xs_ref, wg_ref, wu_ref, wo_ref, y_ref, xb_ref, acc_ref):
    i = pl.program_id(0)
    j = pl.program_id(1)
    last = pl.num_programs(1) - 1
    active = ta_ref[i] == 1

    @pl.when(jnp.logical_and(active, j == 0))
    def _():
        xb_ref[...] = xs_ref[...].astype(BF16)
        acc_ref[...] = jnp.zeros(acc_ref.shape, F32)

    @pl.when(active)
    def _():
        xb = xb_ref[...]
        acc = acc_ref[...]
        for c in range(wg_ref.shape[2] // EXPERT_FF_CHUNK):
            cs = slice(c * EXPERT_FF_CHUNK, (c + 1) * EXPERT_FF_CHUNK)
            h = (_silu(_dot(xb, wg_ref[0, :, cs])) * _dot(xb, wu_ref[0, :, cs])).astype(BF16)
            acc = acc + _dot(h, wo_ref[0, cs, :])
        acc_ref[...] = acc

    @pl.when(jnp.logical_and(active, j == last))
    def _():
        y_ref[...] = acc_ref[...]

    @pl.when(jnp.logical_and(jnp.logical_not(active), j == last))
    def _():
        y_ref[...] = jnp.zeros(y_ref.shape, F32)


def _experts(tile_expert, tile_active, xs, w_in, w_out):
    n_sorted, D = xs.shape
    tm = EXPERT_TILE
    tf = EXPERT_FF_TILE
    dff = w_out.shape[1]
    nf = dff // tf

    def jj(i, j, ta):
        return jnp.where(ta[i] == 1, j, nf - 1)

    grid_spec = pltpu.PrefetchScalarGridSpec(
        num_scalar_prefetch=2,
        grid=(n_sorted // tm, nf),
        in_specs=[
            pl.BlockSpec((tm, D), lambda i, j, te, ta: (i, 0)),
            pl.BlockSpec((1, D, tf), lambda i, j, te, ta: (te[i], 0, jj(i, j, ta))),
            pl.BlockSpec((1, D, tf), lambda i, j, te, ta: (te[i], 0, nf + jj(i, j, ta))),
            pl.BlockSpec((1, tf, D), lambda i, j, te, ta: (te[i], jj(i, j, ta), 0)),
        ],
        out_specs=pl.BlockSpec((tm, D), lambda i, j, te, ta: (i, 0)),
        scratch_shapes=[pltpu.VMEM((tm, D), BF16), pltpu.VMEM((tm, D), F32)],
    )
    return pl.pallas_call(
        _expert_kernel,
        out_shape=jax.ShapeDtypeStruct((n_sorted, D), F32),
        grid_spec=grid_spec,
        compiler_params=_cparams("arbitrary", "arbitrary"),
        name="moe_experts",
    )(tile_expert, tile_active, xs, w_in, w_in, w_out)


def _combine_kernel(pos_ref, ys_ref, g_ref, x_ref, mod_ref, gpost_ref, xo_ref, buf_a, buf_b, sem):
    tm = x_ref.shape[1]

    def issue(r, carry):
        _row_copy(ys_ref, pos_ref[0, 0, r], buf_a, r, sem.at[0]).start()
        _row_copy(ys_ref, pos_ref[0, 1, r], buf_b, r, sem.at[1]).start()
        return carry

    lax.fori_loop(0, tm, issue, 0, unroll=ROW_DMA_UNROLL)
    pltpu.make_async_copy(ys_ref.at[pl.ds(0, tm), :], buf_a, sem.at[0]).wait()
    pltpu.make_async_copy(ys_ref.at[pl.ds(0, tm), :], buf_b, sem.at[1]).wait()
    g = g_ref[0]
    fo = g[:, 0:1] * buf_a[...] + g[:, 1:2] * buf_b[...]
    xo_ref[0] = x_ref[0] + mod_ref[0][5:6, :] * _rms(fo, gpost_ref[...])


def _combine(pos, ys, gates, X, mod, gpost, n_lat, n_rows):
    B, _, D = X.shape
    tm = ROW_DMA_TILE
    return pl.pallas_call(
        _combine_kernel,
        out_shape=jax.ShapeDtypeStruct((B, n_rows, D), F32),
        grid=(B, n_rows // tm),
        in_specs=[
            pl.BlockSpec((1, 2, tm), lambda b, t: (b, 0, t), memory_space=pltpu.SMEM),
            pl.BlockSpec(memory_space=pl.ANY),
            _tok_spec(tm, 2),
            _tok_spec(tm, D),
            _mod_spec(n_lat // tm, B, D),
            _const_spec((1, D)),
        ],
        out_specs=_tok_spec(tm, D),
        scratch_shapes=[pltpu.VMEM((tm, D), F32), pltpu.VMEM((tm, D), F32), pltpu.SemaphoreType.DMA((2,))],
        compiler_params=_cparams("arbitrary", "arbitrary"),
        name="moe_combine",
    )(pos, ys, gates, X, mod, gpost.reshape(1, D))


def _n_expert_tiles(n_tokens, n_experts):
    return (2 * n_tokens) // EXPERT_TILE + n_experts


def _moe(f, logits_t, w_in, w_out, xs_zeros, X, mod, gpost, n_lat):
    B, n, D = f.shape
    ne = logits_t.shape[1]
    tm = EXPERT_TILE
    ints, gates, counts = _route(logits_t)
    counts = counts[:, 0].astype(jnp.int32)
    padded = ((counts + tm - 1) // tm) * tm
    ends = jnp.cumsum(padded)
    starts = ends - padded
    n_tiles = _n_expert_tiles(B * n, ne)
    def start_of(e):
        return sum(jnp.where(e == j, starts[j], 0) for j in range(ne))

    pos = jnp.stack([start_of(ints[:, 0, :]) + ints[:, 2, :], start_of(ints[:, 1, :]) + ints[:, 3, :]], axis=1)
    tile_start = jnp.arange(n_tiles, dtype=jnp.int32) * tm
    tile_active = (tile_start < ends[-1]).astype(jnp.int32)
    tile_expert = jnp.sum((tile_start[:, None] >= ends[None, :]).astype(jnp.int32), axis=1)
    last_expert = jnp.sum((ends[-1] - 1 >= ends).astype(jnp.int32))
    tile_expert = jnp.where(tile_active == 1, tile_expert, last_expert).astype(jnp.int32)
    xs = _dispatch(pos, f, xs_zeros)
    ys = _experts(tile_expert, tile_active, xs, w_in, w_out)
    g2 = jnp.transpose(gates[:, 0:2, :], (0, 2, 1))
    return _combine(pos, ys, g2, X, mod, gpost, n_lat, n)


def _rope_tables(n_lat, n_ctx, rot_dim):
    rows = n_lat // GRID_W
    n_freq = rot_dim // 4
    inv_freq = ROPE_THETA ** (-jnp.arange(n_freq, dtype=F32) / n_freq)
    row_ang = jnp.arange(rows, dtype=F32)[:, None] * inv_freq
    col_ang = jnp.arange(GRID_W, dtype=F32)[:, None] * inv_freq

    def per_position(fn):
        r = fn(row_ang)
        c = fn(col_ang)
        return jnp.concatenate([jnp.repeat(jnp.concatenate([r, r], axis=1), GRID_W, axis=0),
                                jnp.tile(jnp.concatenate([c, c], axis=1), (rows, 1))], axis=1)

    cos = per_position(jnp.cos)
    sin = per_position(jnp.sin)
    first = (jnp.arange(rot_dim) % (2 * n_freq)) < n_freq
    sin_lo = jnp.where(first[None, :], -sin, 0.0)
    sin_hi = jnp.where(first[None, :], 0.0, sin)

    def finish(t, ctx_value):
        t = jnp.concatenate([t, jnp.full((n_ctx, rot_dim), ctx_value, F32)], axis=0)
        return jnp.pad(t, ((0, 0), (0, LANES - rot_dim)))

    return finish(cos, 1.0), finish(sin_lo, 0.0), finish(sin_hi, 0.0)


def kernel(x, c, ctx, c_ctx, ada_w, ada_b, g_pre_mix, g_post_mix, g_pre_ffn, g_post_ffn, ab_w_in, a_norm_g, a_ws,
           a_bs, b_qnorm_g, b_knorm_g, ab_w_out, ffn_w_in, ffn_w_out, mla_w_dq, mla_qnorm_g, mla_w_uq, mla_w_dkv,
           mla_kvnorm_g, mla_w_ukv, mla_w_o, moe_router, moe_w_in, moe_w_out):
    B, S, D = x.shape
    NC = ctx.shape[1]
    depth = ada_w.shape[0]
    assert S % TOKEN_TILE == 0 and S % GRID_W == 0 and NC % ROW_DMA_TILE == 0 and TOKEN_TILE % NC == 0

    X = None
    cpad = jnp.zeros((8, D), F32).at[:B].set(c).at[B].set(c_ctx)
    mods = _adaln(cpad, ada_w, ada_b)
    tables_b = _rope_tables(S, NC, B_HEAD_DIM)
    tables_c = _rope_tables(S, NC, C_ROPE)
    moe_w_out_b = None

    for layer in range(depth):
        need_ctx = layer < depth - 1
        i = layer // 2
        mod = mods[layer]
        n_rows = S + NC if need_ctx else S
        if layer % 2 == 0:
            outs = _ab_in(X, mod, g_pre_mix[layer], ab_w_in[i].astype(BF16), a_norm_g[i], b_qnorm_g[i],
                          b_knorm_g[i], tables_b, S, parts=(x, ctx) if layer == 0 else None)
            u, vg, q, k, v = outs[:5]
            if layer == 0:
                X = outs[5]
            cast = (moe_w_out, i) if layer + 1 < depth else None
            o, moe_w_out_b, _ = _attention(q, k, v, n_lat=S, n_ctx=NC, need_ctx=True, name="gqa",
                                        n_kv_heads=B_KV_HEADS, group=B_HEADS // B_KV_HEADS, dk=B_HEAD_DIM,
                                        dv=B_HEAD_DIM, tq=512, cast_w=cast)
            bias = jnp.broadcast_to(a_bs[i].T[:, :, None], (CHUNK, A_GROUPS, A_GROUP_DIM)).reshape(CHUNK, A_WIDTH)
            X, f = _ab_out(u, vg, o, a_ws[i].astype(BF16), bias, ab_w_out[i].astype(BF16), X, mod,
                           g_post_mix[layer], g_pre_ffn[layer], S)
            X = _ffn(f, ffn_w_in[i].astype(BF16), ffn_w_out[i].astype(BF16), X, mod, g_post_ffn[layer], S)
        else:
            hw = 2 * LANES
            wuq = mla_w_uq[i].reshape(-1, C_HEADS, C_NOPE + C_ROPE)
            wuq = jnp.pad(wuq, ((0, 0), (0, 0), (0, hw - C_NOPE - C_ROPE))).reshape(-1, C_HEADS * hw)
            wdkv = jnp.pad(mla_w_dkv[i], ((0, 0), (0, LANES - C_ROPE)))
            wukv = mla_w_ukv[i].reshape(C_KV_RANK, C_HEADS, C_NOPE + C_V)
            wuk = wukv[:, :, :C_NOPE].reshape(C_KV_RANK, C_HEADS * C_NOPE)
            wuv = wukv[:, :, C_NOPE:].reshape(C_KV_RANK, C_HEADS * C_V)
            q, k, v = _mla_in(X, mod, g_pre_mix[layer], mla_w_dq[i].astype(BF16), mla_qnorm_g[i],
                              wuq.astype(BF16), wdkv.astype(BF16), mla_kvnorm_g[i], wuk.astype(BF16),
                              wuv.astype(BF16), tables_c, S)
            n_sorted = _n_expert_tiles(B * n_rows, moe_router.shape[-1]) * EXPERT_TILE
            o, moe_w_in_b, xs_zeros = _attention(q, k, v, n_lat=S, n_ctx=NC, need_ctx=need_ctx, name="mla",
                                                 n_kv_heads=C_HEADS, group=1, dk=hw, dv=C_V, tq=1024,
                                                 cast_w=(moe_w_in, i), zeros_shape=(n_sorted, D))
            X, f, logits_t = _mla_out(o, mla_w_o[i].astype(BF16), X, mod, g_post_mix[layer], g_pre_ffn[layer],
                                      moe_router[i].T, S, n_rows)
            X = _moe(f, logits_t, moe_w_in_b, moe_w_out_b, xs_zeros, X, mod, g_post_ffn[layer], S)
    return X[:, :S] if X.shape[1] != S else X
```

```python
import functools
import math

import jax
import jax.numpy as jnp
from jax import lax
from jax.experimental import pallas as pl
from jax.experimental.pallas import tpu as pltpu

F32 = jnp.float32
BF16 = jnp.bfloat16

GRID_W = 64
ROPE_THETA = 10000.0
EPS = 1e-6
CHUNK = 128
A_GROUPS = 4
A_GROUP_DIM = 128
A_WIDTH = A_GROUPS * A_GROUP_DIM
B_HEADS = 4
B_KV_HEADS = 2
B_HEAD_DIM = 128
C_HEADS = 8
C_KV_RANK = 256
C_NOPE = 128
C_ROPE = 64
C_V = 128
N_EXPERTS = 8
LANES = 128

VMEM_LIMIT_BYTES = 56 * 1024 * 1024
TOKEN_TILE = 512
TOKEN_SUBTILE = 256
ROW_DMA_TILE = 256
FLASH_KV_CHUNK = 2816
EXPERT_TILE = 512
EXPERT_FF_TILE = 1792
EXPERT_FF_CHUNK = 256
ROW_DMA_UNROLL = 8
FFN_CHUNK = 256
LOG2E = math.log2(math.e)


def _cparams(*sem):
    return pltpu.CompilerParams(dimension_semantics=sem, vmem_limit_bytes=VMEM_LIMIT_BYTES)


def _dot(a, b):
    return jnp.dot(a, b, preferred_element_type=F32)


def _dot_nt(a, b):
    return lax.dot_general(a, b, (((1,), (1,)), ((), ())), preferred_element_type=F32)


def _rms(x, g):
    return x * lax.rsqrt(jnp.mean(x * x, axis=-1, keepdims=True) + EPS) * g


def _gelu(x):
    c = math.sqrt(2.0 / math.pi)
    return x * (0.5 * (1.0 + jnp.tanh(c * (x + 0.044715 * (x * x * x)))))


def _silu(x):
    return x * (1.0 / (1.0 + jnp.exp(-x)))


def _rope(x, cos, sin_lo, sin_hi, quarter):
    return (x * cos + pltpu.roll(x, LANES - quarter, 1) * sin_lo
            + pltpu.roll(x, quarter, 1) * sin_hi)


def _modulated(x, g, mod, shift_row, scale_row):
    return (_rms(x, g) * (1.0 + mod[scale_row:scale_row + 1, :])
            + mod[shift_row:shift_row + 1, :])


def _adaln_kernel(c_ref, w_ref, b_ref, o_ref):
    act = _silu(c_ref[...]).astype(BF16)
    o_ref[0, 0] = _dot(act, w_ref[0].astype(BF16)) + b_ref[0, 0]


def _adaln(cpad, ada_w, ada_b):
    depth, d, _ = ada_w.shape
    rows = cpad.shape[0]
    out = pl.pallas_call(
        _adaln_kernel,
        out_shape=jax.ShapeDtypeStruct((depth, 6, rows, d), F32),
        grid=(depth, 6),
        in_specs=[
            pl.BlockSpec((rows, d), lambda l, j: (0, 0)),
            pl.BlockSpec((1, d, d), lambda l, j: (l, 0, j)),
            pl.BlockSpec((1, 1, 1, d), lambda l, j: (l, j, 0, 0)),
        ],
        out_specs=pl.BlockSpec((1, 1, rows, d), lambda l, j: (l, j, 0, 0)),
        compiler_params=_cparams("arbitrary", "arbitrary"),
        name="adaln",
    )(cpad, ada_w, ada_b.reshape(depth, 6, 1, d))
    return jnp.transpose(out, (0, 2, 1, 3))


def _tok_spec(tm, width):
    return pl.BlockSpec((1, tm, width), lambda b, t: (b, t, 0))


def _mod_spec(n_lat_tiles, ctx_row, d):
    return pl.BlockSpec((1, 6, d), lambda b, t: (jnp.where(t >= n_lat_tiles, ctx_row, b), 0, 0))


def _const_spec(shape):
    nd = len(shape)
    return pl.BlockSpec(shape, lambda b, t: (0,) * nd)


def _table_spec(tm):
    return pl.BlockSpec((tm, LANES), lambda b, t: (t, 0))


def _ab_in_kernel(*refs, qscale, n_lat_tiles, from_parts):
    if from_parts:
        (x_ref, ctx_ref, mod_ref, gpre_ref, w_ref, ang_ref, qg_ref, kg_ref, cos_ref, slo_ref, shi_ref,
         u_ref, vg_ref, q_ref, k_ref, v_ref, xo_ref) = refs
        reps = x_ref.shape[1] // ctx_ref.shape[1]
        ctx_tile = jnp.concatenate([ctx_ref[0]] * reps, axis=0)
        xo_ref[0] = jnp.where(pl.program_id(1) >= n_lat_tiles, ctx_tile, x_ref[0])
        x_ref = xo_ref
    else:
        (x_ref, mod_ref, gpre_ref, w_ref, ang_ref, qg_ref, kg_ref, cos_ref, slo_ref, shi_ref,
         u_ref, vg_ref, q_ref, k_ref, v_ref) = refs
    quarter = B_HEAD_DIM // 4
    c0 = 2 * A_WIDTH
    c1 = c0 + B_HEADS * B_HEAD_DIM
    c2 = c1 + B_KV_HEADS * B_HEAD_DIM
    tm = x_ref.shape[1]
    for r0 in range(0, tm, TOKEN_SUBTILE):
        rows = slice(r0, r0 + TOKEN_SUBTILE)
        hb = _modulated(x_ref[0, rows, :], gpre_ref[...], mod_ref[0], 0, 1).astype(BF16)
        cos = cos_ref[rows, :]
        slo = slo_ref[rows, :]
        shi = shi_ref[rows, :]
        u_ref[0, rows, :] = _gelu(_dot(hb, w_ref[:, 0:A_WIDTH])).astype(BF16)
        v = _gelu(_dot(hb, w_ref[:, A_WIDTH:c0]))
        for g in range(A_GROUPS):
            sl = slice(g * A_GROUP_DIM, (g + 1) * A_GROUP_DIM)
            vg_ref[0, rows, sl] = _rms(v[:, sl], ang_ref[:, sl]).astype(BF16)
        q = _dot(hb, w_ref[:, c0:c1])
        for h in range(B_HEADS):
            sl = slice(h * B_HEAD_DIM, (h + 1) * B_HEAD_DIM)
            qr = _rope(_rms(q[:, sl], qg_ref[...]), cos, slo, shi, quarter)
            q_ref[0, rows, sl] = (qr * qscale).astype(BF16)
        k = _dot(hb, w_ref[:, c1:c2])
        for h in range(B_KV_HEADS):
            sl = slice(h * B_HEAD_DIM, (h + 1) * B_HEAD_DIM)
            k_ref[0, rows, sl] = _rope(_rms(k[:, sl], kg_ref[...]), cos, slo, shi, quarter).astype(BF16)
        v_ref[0, rows, :] = _dot(hb, w_ref[:, c2:c2 + B_KV_HEADS * B_HEAD_DIM]).astype(BF16)


def _ab_in(X, mod, gpre, w_in, a_norm_g, qg, kg, tables, n_lat, parts=None):
    tm = TOKEN_TILE
    n_lat_tiles = n_lat // tm
    if parts is None:
        B, T, D = X.shape
        srcs = [X]
        src_specs = [_tok_spec(tm, D)]
    else:
        lat, ctx_in = parts
        B, _, D = lat.shape
        T = n_lat + ctx_in.shape[1]
        srcs = [lat, ctx_in]
        src_specs = [pl.BlockSpec((1, tm, D), lambda b, t: (b, jnp.minimum(t, n_lat_tiles - 1), 0)),
                     pl.BlockSpec((1, ctx_in.shape[1], D), lambda b, t: (b, 0, 0))]
    qw = B_HEADS * B_HEAD_DIM
    kw = B_KV_HEADS * B_HEAD_DIM
    qscale = (B_HEAD_DIM ** -0.5) * LOG2E
    outs = pl.pallas_call(
        functools.partial(_ab_in_kernel, qscale=qscale, n_lat_tiles=n_lat_tiles, from_parts=parts is not None),
        out_shape=[
            jax.ShapeDtypeStruct((B, T, A_WIDTH), BF16),
            jax.ShapeDtypeStruct((B, T, A_WIDTH), BF16),
            jax.ShapeDtypeStruct((B, T, qw), BF16),
            jax.ShapeDtypeStruct((B, T, kw), BF16),
            jax.ShapeDtypeStruct((B, T, kw), BF16),
        ] + ([jax.ShapeDtypeStruct((B, T, D), F32)] if parts is not None else []),
        grid=(B, pl.cdiv(T, tm)),
        in_specs=src_specs + [
            _mod_spec(n_lat_tiles, B, D),
            _const_spec((1, D)),
            _const_spec(w_in.shape),
            _const_spec((1, A_WIDTH)),
            _const_spec((1, B_HEAD_DIM)),
            _const_spec((1, B_HEAD_DIM)),
            _table_spec(tm), _table_spec(tm), _table_spec(tm),
        ],
        out_specs=[_tok_spec(tm, A_WIDTH), _tok_spec(tm, A_WIDTH), _tok_spec(tm, qw),
                   _tok_spec(tm, kw), _tok_spec(tm, kw)] + ([_tok_spec(tm, D)] if parts is not None else []),
        compiler_params=_cparams("arbitrary", "arbitrary"),
        name="ab_in",
    )(*srcs, mod, gpre.reshape(1, D), w_in, a_norm_g.reshape(1, A_WIDTH), qg.reshape(1, -1), kg.reshape(1, -1),
      *tables)
    return outs


def _flash_kernel(*refs, group, dk, dv, kv_len, tk, with_cast, with_zeros):
    refs = list(refs)
    q_ref, k_ref, v_ref = refs[:3]
    w_ref = refs[3] if with_cast else None
    outs = refs[3 + with_cast:]
    o_ref = outs.pop(0)
    if with_cast:
        outs.pop(0)[...] = w_ref[...].astype(BF16)
    if with_zeros:
        z_ref = outs.pop(0)
        z_ref[...] = jnp.zeros(z_ref.shape, z_ref.dtype)
    m_ref, l_ref, acc_ref = outs
    tq = q_ref.shape[1]
    if group == 1:
        q = q_ref[0]
    else:
        q = jnp.concatenate([q_ref[0, :, g * dk:(g + 1) * dk] for g in range(group)], axis=0)
    m_ref[...] = jnp.full(m_ref.shape, -jnp.inf, F32)
    l_ref[...] = jnp.zeros(l_ref.shape, F32)
    acc_ref[...] = jnp.zeros(acc_ref.shape, F32)

    for start in range(0, kv_len, tk):
        size = min(tk, kv_len - start)
        s = _dot_nt(q, k_ref[0, start:start + size, :])
        m_prev = m_ref[...]
        m_new = jnp.maximum(m_prev, jnp.max(s, axis=-1, keepdims=True))
        alpha = jnp.exp2(m_prev - m_new)
        p = [jnp.exp2(s[:, t * LANES:(t + 1) * LANES] - m_new) for t in range(size // LANES)]
        l_ref[...] = alpha * l_ref[...] + functools.reduce(lambda a, b: a + b, p)
        pb = jnp.concatenate([t.astype(BF16) for t in p], axis=1)
        acc_ref[...] = alpha * acc_ref[...] + _dot(pb, v_ref[0, start:start + size, :])
        m_ref[...] = m_new

    out = acc_ref[...] * (1.0 / jnp.sum(l_ref[...], axis=-1, keepdims=True))
    for g in range(group):
        o_ref[0, :, g * dv:(g + 1) * dv] = out[g * tq:(g + 1) * tq].astype(o_ref.dtype)


def _flash(q, k, v, *, n_kv_heads, group, dk, dv, tq, q_row0, n_q_rows, kv_row0, kv_len, name, cast_w=None,
           zeros_shape=None):
    B = q.shape[0]
    q_blk0 = q_row0 // tq
    kv_blk0 = kv_row0 // kv_len
    tk = min(FLASH_KV_CHUNK, kv_len)
    n_q = n_q_rows // tq
    n_steps = B * n_kv_heads * n_q

    def step(b, h, i):
        return (b * n_kv_heads + h) * n_q + i

    in_specs = [
        pl.BlockSpec((1, tq, group * dk), lambda b, h, i: (b, q_blk0 + i, h)),
        pl.BlockSpec((1, kv_len, dk), lambda b, h, i: (b, kv_blk0, h)),
        pl.BlockSpec((1, kv_len, dv), lambda b, h, i: (b, kv_blk0, h)),
    ]
    out_specs = [pl.BlockSpec((1, tq, group * dv), lambda b, h, i: (b, i, h))]
    out_shape = [jax.ShapeDtypeStruct((B, n_q_rows, n_kv_heads * group * dv), BF16)]
    args = [q, k, v]
    if cast_w is not None:
        w_all, li = cast_w
        slabs = w_all.reshape(w_all.shape[0] * n_steps, -1, w_all.shape[-1])
        blk = (1,) + slabs.shape[1:]
        in_specs.append(pl.BlockSpec(blk, lambda b, h, i: (li * n_steps + step(b, h, i), 0, 0)))
        out_specs.append(pl.BlockSpec(blk, lambda b, h, i: (step(b, h, i), 0, 0)))
        out_shape.append(jax.ShapeDtypeStruct((n_steps,) + slabs.shape[1:], BF16))
        args.append(slabs)
    if zeros_shape is not None:
        rows, cols = zeros_shape
        out_specs.append(pl.BlockSpec((rows // n_steps, cols), lambda b, h, i: (step(b, h, i), 0)))
        out_shape.append(jax.ShapeDtypeStruct((rows, cols), F32))
    outs = pl.pallas_call(
        functools.partial(_flash_kernel, group=group, dk=dk, dv=dv, kv_len=kv_len, tk=tk,
                          with_cast=cast_w is not None, with_zeros=zeros_shape is not None),
        out_shape=out_shape,
        grid=(B, n_kv_heads, n_q),
        in_specs=in_specs,
        out_specs=out_specs,
        scratch_shapes=[
            pltpu.VMEM((group * tq, LANES), F32),
            pltpu.VMEM((group * tq, LANES), F32),
            pltpu.VMEM((group * tq, dv), F32),
        ],
        compiler_params=_cparams("arbitrary", "arbitrary", "arbitrary"),
        name=name,
    )(*args)
    outs = list(outs)
    o = outs.pop(0)
    w_b = outs.pop(0).reshape(cast_w[0].shape[1:]) if cast_w is not None else None
    zeros = outs.pop(0) if zeros_shape is not None else None
    return o, w_b, zeros


def _attention(q, k, v, *, n_lat, n_ctx, need_ctx, name, cast_w=None, zeros_shape=None, **kw):
    T = n_lat + n_ctx
    o, w_b, zeros = _flash(q, k, v, q_row0=0, n_q_rows=n_lat, kv_row0=0, kv_len=T, name=name + "_lat",
                           cast_w=cast_w, zeros_shape=zeros_shape, **kw)
    if need_ctx:
        kw = dict(kw, tq=min(kw["tq"], n_ctx))
        o_ctx, _, _ = _flash(q, k, v, q_row0=n_lat, n_q_rows=n_ctx, kv_row0=n_lat, kv_len=n_ctx,
                             name=name + "_ctx", **kw)
        o = jnp.concatenate([o, o_ctx], axis=1)
    return o, w_b, zeros


def _post_mix(y, x, mod, gpost, gpre):
    xn = x + mod[2:3, :] * _rms(y, gpost)
    return xn, _modulated(xn, gpre, mod, 3, 4)


def _ab_out_kernel(u_ref, vg_ref, o_ref, ws_ref, bias_ref, wout_ref, x_ref, mod_ref, gpost_ref, gpre_ref,
                   xo_ref, f_ref):
    tm = u_ref.shape[1]
    y = _dot(o_ref[0], wout_ref[A_WIDTH:, :])
    gated = []
    for c in range(tm // CHUNK):
        rows = slice(c * CHUNK, (c + 1) * CHUNK)
        mixed = jnp.concatenate(
            [_dot(ws_ref[g], vg_ref[0, rows, g * A_GROUP_DIM:(g + 1) * A_GROUP_DIM]) for g in range(A_GROUPS)],
            axis=1) + bias_ref[...]
        gated.append((u_ref[0, rows, :].astype(F32) * mixed).astype(BF16))
    y = y + _dot(jnp.concatenate(gated, axis=0), wout_ref[:A_WIDTH, :])
    xn, f = _post_mix(y, x_ref[0], mod_ref[0], gpost_ref[...], gpre_ref[...])
    xo_ref[0] = xn
    f_ref[0] = f.astype(f_ref.dtype)


def _ab_out(u, vg, o, ws, bias, w_out, X, mod, gpost, gpre, n_lat):
    B, T, D = X.shape
    tm = TOKEN_TILE
    return pl.pallas_call(
        _ab_out_kernel,
        out_shape=[jax.ShapeDtypeStruct((B, T, D), F32), jax.ShapeDtypeStruct((B, T, D), BF16)],
        grid=(B, pl.cdiv(T, tm)),
        in_specs=[
            _tok_spec(tm, A_WIDTH), _tok_spec(tm, A_WIDTH), _tok_spec(tm, o.shape[2]),
            _const_spec(ws.shape), _const_spec(bias.shape), _const_spec(w_out.shape),
            _tok_spec(tm, D), _mod_spec(n_lat // tm, B, D), _const_spec((1, D)), _const_spec((1, D)),
        ],
        out_specs=[_tok_spec(tm, D), _tok_spec(tm, D)],
        compiler_params=_cparams("arbitrary", "arbitrary"),
        name="ab_out",
    )(u, vg, o, ws, bias, w_out, X, mod, gpost.reshape(1, D), gpre.reshape(1, D))


def _ffn_kernel(f_ref, win_ref, wo_ref, x_ref, mod_ref, gpost_ref, xo_ref):
    dff = wo_ref.shape[0]
    fb = f_ref[0]
    acc = None
    for c0 in range(0, dff, FFN_CHUNK):
        gate = _dot(fb, win_ref[:, c0:c0 + FFN_CHUNK])
        up = _dot(fb, win_ref[:, dff + c0:dff + c0 + FFN_CHUNK])
        part = _dot((_silu(gate) * up).astype(BF16), wo_ref[c0:c0 + FFN_CHUNK, :])
        acc = part if acc is None else acc + part
    xo_ref[0] = x_ref[0] + mod_ref[0][5:6, :] * _rms(acc, gpost_ref[...])


def _ffn(f, w_in, w_out, X, mod, gpost, n_lat):
    B, T, D = X.shape
    tm = TOKEN_TILE
    return pl.pallas_call(
        _ffn_kernel,
        out_shape=jax.ShapeDtypeStruct((B, T, D), F32),
        grid=(B, pl.cdiv(T, tm)),
        in_specs=[
            _tok_spec(tm, D), _const_spec(w_in.shape), _const_spec(w_out.shape),
            _tok_spec(tm, D), _mod_spec(n_lat // tm, B, D), _const_spec((1, D)),
        ],
        out_specs=_tok_spec(tm, D),
        compiler_params=_cparams("arbitrary", "arbitrary"),
        name="ffn",
    )(f, w_in, w_out, X, mod, gpost.reshape(1, D))


def _mla_in_kernel(x_ref, mod_ref, gpre_ref, wdq_ref, qng_ref, wuq_ref, wdkv_ref, kvng_ref, wuk_ref, wuv_ref,
                   cos_ref, slo_ref, shi_ref, q_ref, k_ref, v_ref, *, qscale):
    hb = _modulated(x_ref[0], gpre_ref[...], mod_ref[0], 0, 1).astype(BF16)
    cos = cos_ref[...]
    slo = slo_ref[...]
    shi = shi_ref[...]
    quarter = C_ROPE // 4
    hw = 2 * LANES
    cq = _rms(_dot(hb, wdq_ref[...]), qng_ref[...]).astype(BF16)
    qf = _dot(cq, wuq_ref[...])
    for h in range(C_HEADS):
        q_ref[0, :, h * hw:h * hw + LANES] = (qf[:, h * hw:h * hw + LANES] * qscale).astype(BF16)
        qr = _rope(qf[:, h * hw + LANES:(h + 1) * hw], cos, slo, shi, quarter)
        q_ref[0, :, h * hw + LANES:(h + 1) * hw] = (qr * qscale).astype(BF16)
    kvin = _dot(hb, wdkv_ref[...])
    ckv = _rms(kvin[:, :C_KV_RANK], kvng_ref[...]).astype(BF16)
    krope = _rope(kvin[:, C_KV_RANK:], cos, slo, shi, quarter).astype(BF16)
    kn = _dot(ckv, wuk_ref[...])
    for h in range(C_HEADS):
        k_ref[0, :, h * hw:h * hw + LANES] = kn[:, h * C_NOPE:(h + 1) * C_NOPE].astype(BF16)
        k_ref[0, :, h * hw + LANES:(h + 1) * hw] = krope
    v_ref[0] = _dot(ckv, wuv_ref[...]).astype(BF16)


def _mla_in(X, mod, gpre, wdq, qng, wuq, wdkv, kvng, wuk, wuv, tables, n_lat):
    B, T, D = X.shape
    tm = TOKEN_TILE
    qscale = ((C_NOPE + C_ROPE) ** -0.5) * LOG2E
    hw = 2 * LANES
    return pl.pallas_call(
        functools.partial(_mla_in_kernel, qscale=qscale),
        out_shape=[
            jax.ShapeDtypeStruct((B, T, C_HEADS * hw), BF16),
            jax.ShapeDtypeStruct((B, T, C_HEADS * hw), BF16),
            jax.ShapeDtypeStruct((B, T, C_HEADS * C_V), BF16),
        ],
        grid=(B, pl.cdiv(T, tm)),
        in_specs=[
            _tok_spec(tm, D), _mod_spec(n_lat // tm, B, D), _const_spec((1, D)),
            _const_spec(wdq.shape), _const_spec((1, qng.shape[0])), _const_spec(wuq.shape),
            _const_spec(wdkv.shape), _const_spec((1, kvng.shape[0])), _const_spec(wuk.shape),
            _const_spec(wuv.shape),
            _table_spec(tm), _table_spec(tm), _table_spec(tm),
        ],
        out_specs=[_tok_spec(tm, C_HEADS * hw), _tok_spec(tm, C_HEADS * hw), _tok_spec(tm, C_HEADS * C_V)],
        compiler_params=_cparams("arbitrary", "arbitrary"),
        name="mla_in",
    )(X, mod, gpre.reshape(1, D), wdq, qng.reshape(1, -1), wuq, wdkv, kvng.reshape(1, -1), wuk, wuv, *tables)


def _split_bf16(x):
    hi = x.astype(BF16)
    lo = (x - hi.astype(F32)).astype(BF16)
    return hi, lo


def _mla_out_kernel(o_ref, wo_ref, x_ref, mod_ref, gpost_ref, gpre_ref, rt_ref, xo_ref, f_ref, lg_ref):
    y = _dot(o_ref[0], wo_ref[...])
    xn, f = _post_mix(y, x_ref[0], mod_ref[0], gpost_ref[...], gpre_ref[...])
    xo_ref[0] = xn
    f_ref[0] = f
    f_hi, f_lo = _split_bf16(f)
    r_hi, r_lo = _split_bf16(rt_ref[...])
    ne = r_hi.shape[0]
    d = r_hi.shape[1]
    a = _dot_nt(f_hi, jnp.concatenate([r_hi, r_lo, jnp.zeros((LANES - 2 * ne, d), BF16)], axis=0))
    b = _dot_nt(f_lo, jnp.concatenate([r_hi, jnp.zeros((LANES - ne, d), BF16)], axis=0))
    logits = a + (pltpu.roll(a, LANES - ne, 1) + b)
    lg_ref[0] = logits.T[0:ne, :]


def _mla_out(o, w_o, X, mod, gpost, gpre, router_t, n_lat, n_rows):
    B, _, D = X.shape
    tm = TOKEN_TILE
    ne = router_t.shape[0]
    return pl.pallas_call(
        _mla_out_kernel,
        out_shape=[jax.ShapeDtypeStruct((B, n_rows, D), F32), jax.ShapeDtypeStruct((B, n_rows, D), F32),
                   jax.ShapeDtypeStruct((B, ne, n_rows), F32)],
        grid=(B, pl.cdiv(n_rows, tm)),
        in_specs=[
            _tok_spec(tm, D), _const_spec(w_o.shape), _tok_spec(tm, D), _mod_spec(n_lat // tm, B, D),
            _const_spec((1, D)), _const_spec((1, D)), _const_spec(router_t.shape),
        ],
        out_specs=[_tok_spec(tm, D), _tok_spec(tm, D), pl.BlockSpec((1, ne, tm), lambda b, t: (b, 0, t))],
        compiler_params=_cparams("arbitrary", "arbitrary"),
        name="mla_out",
    )(o, w_o, X, mod, gpost.reshape(1, D), gpre.reshape(1, D), router_t)


def _route_kernel(lg_ref, ints_ref, gates_ref, cnt_ref, carry_ref):
    first = jnp.logical_and(pl.program_id(0) == 0, pl.program_id(1) == 0)

    @pl.when(first)
    def _():
        carry_ref[...] = jnp.zeros(carry_ref.shape, F32)

    lg = lg_ref[0]
    ne, tn = lg.shape
    eidx = lax.broadcasted_iota(jnp.int32, (ne, tn), 0)
    m1 = jnp.max(lg, axis=0, keepdims=True)
    i1 = jnp.min(jnp.where(lg == m1, eidx, ne), axis=0, keepdims=True)
    rest = jnp.where(eidx == i1, -jnp.inf, lg)
    m2 = jnp.max(rest, axis=0, keepdims=True)
    i2 = jnp.min(jnp.where(rest == m2, eidx, ne), axis=0, keepdims=True)
    e2 = jnp.exp(m2 - m1)
    den = 1.0 + e2
    w1 = 1.0 / den
    w2 = e2 / den
    member = jnp.logical_or(eidx == i1, eidx == i2)
    before = lax.broadcasted_iota(jnp.int32, (tn, tn), 0) < lax.broadcasted_iota(jnp.int32, (tn, tn), 1)
    cum = _dot(member.astype(BF16), before.astype(BF16)) + carry_ref[:, 0:1]
    r1 = jnp.sum(jnp.where(eidx == i1, cum, 0.0), axis=0, keepdims=True)
    r2 = jnp.sum(jnp.where(eidx == i2, cum, 0.0), axis=0, keepdims=True)
    carry_ref[...] = carry_ref[...] + jnp.sum(member.astype(F32), axis=1, keepdims=True)
    row = lax.broadcasted_iota(jnp.int32, (8, tn), 0)
    ints = jnp.where(row == 0, i1, jnp.where(row == 1, i2, jnp.where(row == 2, r1.astype(jnp.int32),
                                                                      r2.astype(jnp.int32))))
    ints_ref[0] = ints
    gates_ref[0] = jnp.where(row == 0, w1, w2)
    cnt_ref[...] = carry_ref[...]


def _route(logits_t):
    B, ne, n = logits_t.shape
    tn = ROW_DMA_TILE
    return pl.pallas_call(
        _route_kernel,
        out_shape=[jax.ShapeDtypeStruct((B, 8, n), jnp.int32), jax.ShapeDtypeStruct((B, 8, n), F32),
                   jax.ShapeDtypeStruct((ne, LANES), F32)],
        grid=(B, n // tn),
        in_specs=[pl.BlockSpec((1, ne, tn), lambda b, t: (b, 0, t))],
        out_specs=[pl.BlockSpec((1, 8, tn), lambda b, t: (b, 0, t)),
                   pl.BlockSpec((1, 8, tn), lambda b, t: (b, 0, t)),
                   pl.BlockSpec((ne, LANES), lambda b, t: (0, 0))],
        scratch_shapes=[pltpu.VMEM((ne, LANES), F32)],
        compiler_params=_cparams("arbitrary", "arbitrary"),
        name="moe_route",
    )(logits_t)


def _row_copy(src_ref, src_row, dst_ref, dst_row, sem):
    return pltpu.make_async_copy(src_ref.at[pl.ds(src_row, 1), :], dst_ref.at[pl.ds(dst_row, 1), :], sem)


def _dispatch_kernel(pos_ref, f_ref, xs_in_ref, xs_ref, sem):
    del xs_in_ref
    tm = f_ref.shape[1]
    src = f_ref.at[0]

    def issue(r, carry):
        _row_copy(src, r, xs_ref, pos_ref[0, 0, r], sem.at[0]).start()
        _row_copy(src, r, xs_ref, pos_ref[0, 1, r], sem.at[1]).start()
        return carry

    lax.fori_loop(0, tm, issue, 0, unroll=ROW_DMA_UNROLL)
    for slot in range(2):
        pltpu.make_async_copy(src, xs_ref.at[pl.ds(0, tm), :], sem.at[slot]).wait()


def _dispatch(pos, f, zeros):
    B, n, D = f.shape
    n_sorted = zeros.shape[0]
    tm = ROW_DMA_TILE
    return pl.pallas_call(
        _dispatch_kernel,
        out_shape=jax.ShapeDtypeStruct((n_sorted, D), f.dtype),
        grid=(B, n // tm),
        in_specs=[
            pl.BlockSpec((1, 2, tm), lambda b, t: (b, 0, t), memory_space=pltpu.SMEM),
            _tok_spec(tm, D),
            pl.BlockSpec(memory_space=pl.ANY),
        ],
        out_specs=pl.BlockSpec(memory_space=pl.ANY),
        scratch_shapes=[pltpu.SemaphoreType.DMA((2,))],
        input_output_aliases={2: 0},
        compiler_params=_cparams("arbitrary", "arbitrary"),
        name="moe_dispatch",
    )(pos, f, zeros)


def _expert_kernel(te_ref, ta_ref, xs_ref, wg_ref, wu_ref, wo_ref, y_ref, xb_ref, acc_ref):
    i = pl.program_id(0)
    j = pl.program_id(1)
    last = pl.num_programs(1) - 1
    active = ta_ref[i] == 1

    @pl.when(jnp.logical_and(active, j == 0))
    def _():
        xb_ref[...] = xs_ref[...].astype(BF16)
        acc_ref[...] = jnp.zeros(acc_ref.shape, F32)

    @pl.when(active)
    def _():
        xb = xb_ref[...]
        acc = acc_ref[...]
        for c in range(wg_ref.shape[2] // EXPERT_FF_CHUNK):
            cs = slice(c * EXPERT_FF_CHUNK, (c + 1) * EXPERT_FF_CHUNK)
            h = (_silu(_dot(xb, wg_ref[0, :, cs])) * _dot(xb, wu_ref[0, :, cs])).astype(BF16)
            acc = acc + _dot(h, wo_ref[0, cs, :])
        acc_ref[...] = acc

    @pl.when(jnp.logical_and(active, j == last))
    def _():
        y_ref[...] = acc_ref[...]

    @pl.when(jnp.logical_and(jnp.logical_not(active), j == last))
    def _():
        y_ref[...] = jnp.zeros(y_ref.shape, F32)


def _experts(tile_expert, tile_active, xs, w_in, w_out):
    n_sorted, D = xs.shape
    tm = EXPERT_TILE
    tf = EXPERT_FF_TILE
    dff = w_out.shape[1]
    nf = dff // tf

    def jj(i, j, ta):
        return jnp.where(ta[i] == 1, j, nf - 1)

    grid_spec = pltpu.PrefetchScalarGridSpec(
        num_scalar_prefetch=2,
        grid=(n_sorted // tm, nf),
        in_specs=[
            pl.BlockSpec((tm, D), lambda i, j, te, ta: (i, 0)),
            pl.BlockSpec((1, D, tf), lambda i, j, te, ta: (te[i], 0, jj(i, j, ta))),
            pl.BlockSpec((1, D, tf), lambda i, j, te, ta: (te[i], 0, nf + jj(i, j, ta))),
            pl.BlockSpec((1, tf, D), lambda i, j, te, ta: (te[i], jj(i, j, ta), 0)),
        ],
        out_specs=pl.BlockSpec((tm, D), lambda i, j, te, ta: (i, 0)),
        scratch_shapes=[pltpu.VMEM((tm, D), BF16), pltpu.VMEM((tm, D), F32)],
    )
    return pl.pallas_call(
        _expert_kernel,
        out_shape=jax.ShapeDtypeStruct((n_sorted, D), F32),
        grid_spec=grid_spec,
        compiler_params=_cparams("arbitrary", "arbitrary"),
        name="moe_experts",
    )(tile_expert, tile_active, xs, w_in, w_in, w_out)


def _combine_kernel(pos_ref, ys_ref, g_ref, x_ref, mod_ref, gpost_ref, xo_ref, buf_a, buf_b, sem):
    tm = x_ref.shape[1]

    def issue(r, carry):
        _row_copy(ys_ref, pos_ref[0, 0, r], buf_a, r, sem.at[0]).start()
        _row_copy(ys_ref, pos_ref[0, 1, r], buf_b, r, sem.at[1]).start()
        return carry

    lax.fori_loop(0, tm, issue, 0, unroll=ROW_DMA_UNROLL)
    pltpu.make_async_copy(ys_ref.at[pl.ds(0, tm), :], buf_a, sem.at[0]).wait()
    pltpu.make_async_copy(ys_ref.at[pl.ds(0, tm), :], buf_b, sem.at[1]).wait()
    g = g_ref[0]
    fo = g[:, 0:1] * buf_a[...] + g[:, 1:2] * buf_b[...]
    xo_ref[0] = x_ref[0] + mod_ref[0][5:6, :] * _rms(fo, gpost_ref[...])


def _combine(pos, ys, gates, X, mod, gpost, n_lat, n_rows):
    B, _, D = X.shape
    tm = ROW_DMA_TILE
    return pl.pallas_call(
        _combine_kernel,
        out_shape=jax.ShapeDtypeStruct((B, n_rows, D), F32),
        grid=(B, n_rows // tm),
        in_specs=[
            pl.BlockSpec((1, 2, tm), lambda b, t: (b, 0, t), memory_space=pltpu.SMEM),
            pl.BlockSpec(memory_space=pl.ANY),
            _tok_spec(tm, 2),
            _tok_spec(tm, D),
            _mod_spec(n_lat // tm, B, D),
            _const_spec((1, D)),
        ],
        out_specs=_tok_spec(tm, D),
        scratch_shapes=[pltpu.VMEM((tm, D), F32), pltpu.VMEM((tm, D), F32), pltpu.SemaphoreType.DMA((2,))],
        compiler_params=_cparams("arbitrary", "arbitrary"),
        name="moe_combine",
    )(pos, ys, gates, X, mod, gpost.reshape(1, D))


def _n_expert_tiles(n_tokens, n_experts):
    return (2 * n_tokens) // EXPERT_TILE + n_experts


def _moe(f, logits_t, w_in, w_out, xs_zeros, X, mod, gpost, n_lat):
    B, n, D = f.shape
    ne = logits_t.shape[1]
    tm = EXPERT_TILE
    ints, gates, counts = _route(logits_t)
    counts = counts[:, 0].astype(jnp.int32)
    padded = ((counts + tm - 1) // tm) * tm
    ends = jnp.cumsum(padded)
    starts = ends - padded
    n_tiles = _n_expert_tiles(B * n, ne)
    def start_of(e):
        return sum(jnp.where(e == j, starts[j], 0) for j in range(ne))

    pos = jnp.stack([start_of(ints[:, 0, :]) + ints[:, 2, :], start_of(ints[:, 1, :]) + ints[:, 3, :]], axis=1)
    tile_start = jnp.arange(n_tiles, dtype=jnp.int32) * tm
    tile_active = (tile_start < ends[-1]).astype(jnp.int32)
    tile_expert = jnp.sum((tile_start[:, None] >= ends[None, :]).astype(jnp.int32), axis=1)
    last_expert = jnp.sum((ends[-1] - 1 >= ends).astype(jnp.int32))
    tile_expert = jnp.where(tile_active == 1, tile_expert, last_expert).astype(jnp.int32)
    xs = _dispatch(pos, f, xs_zeros)
    ys = _experts(tile_expert, tile_active, xs, w_in, w_out)
    g2 = jnp.transpose(gates[:, 0:2, :], (0, 2, 1))
    return _combine(pos, ys, g2, X, mod, gpost, n_lat, n)


def _rope_tables(n_lat, n_ctx, rot_dim):
    rows = n_lat // GRID_W
    n_freq = rot_dim // 4
    inv_freq = ROPE_THETA ** (-jnp.arange(n_freq, dtype=F32) / n_freq)
    row_ang = jnp.arange(rows, dtype=F32)[:, None] * inv_freq
    col_ang = jnp.arange(GRID_W, dtype=F32)[:, None] * inv_freq

    def per_position(fn):
        r = fn(row_ang)
        c = fn(col_ang)
        return jnp.concatenate([jnp.repeat(jnp.concatenate([r, r], axis=1), GRID_W, axis=0),
                                jnp.tile(jnp.concatenate([c, c], axis=1), (rows, 1))], axis=1)

    cos = per_position(jnp.cos)
    sin = per_position(jnp.sin)
    first = (jnp.arange(rot_dim) % (2 * n_freq)) < n_freq
    sin_lo = jnp.where(first[None, :], -sin, 0.0)
    sin_hi = jnp.where(first[None, :], 0.0, sin)

    def finish(t, ctx_value):
        t = jnp.concatenate([t, jnp.full((n_ctx, rot_dim), ctx_value, F32)], axis=0)
        return jnp.pad(t, ((0, 0), (0, LANES - rot_dim)))

    return finish(cos, 1.0), finish(sin_lo, 0.0), finish(sin_hi, 0.0)


def kernel(x, c, ctx, c_ctx, ada_w, ada_b, g_pre_mix, g_post_mix, g_pre_ffn, g_post_ffn, ab_w_in, a_norm_g, a_ws,
           a_bs, b_qnorm_g, b_knorm_g, ab_w_out, ffn_w_in, ffn_w_out, mla_w_dq, mla_qnorm_g, mla_w_uq, mla_w_dkv,
           mla_kvnorm_g, mla_w_ukv, mla_w_o, moe_router, moe_w_in, moe_w_out):
    B, S, D = x.shape
    NC = ctx.shape[1]
    depth = ada_w.shape[0]
    assert S % TOKEN_TILE == 0 and S % GRID_W == 0 and NC % ROW_DMA_TILE == 0 and TOKEN_TILE % NC == 0

    X = None
    cpad = jnp.zeros((8, D), F32).at[:B].set(c).at[B].set(c_ctx)
    mods = _adaln(cpad, ada_w, ada_b)
    tables_b = _rope_tables(S, NC, B_HEAD_DIM)
    tables_c = _rope_tables(S, NC, C_ROPE)
    moe_w_out_b = None

    for layer in range(depth):
        need_ctx = layer < depth - 1
        i = layer // 2
        mod = mods[layer]
        n_rows = S + NC if need_ctx else S
        if layer % 2 == 0:
            outs = _ab_in(X, mod, g_pre_mix[layer], ab_w_in[i].astype(BF16), a_norm_g[i], b_qnorm_g[i],
                          b_knorm_g[i], tables_b, S, parts=(x, ctx) if layer == 0 else None)
            u, vg, q, k, v = outs[:5]
            if layer == 0:
                X = outs[5]
            cast = (moe_w_out, i) if layer + 1 < depth else None
            o, moe_w_out_b, _ = _attention(q, k, v, n_lat=S, n_ctx=NC, need_ctx=True, name="gqa",
                                        n_kv_heads=B_KV_HEADS, group=B_HEADS // B_KV_HEADS, dk=B_HEAD_DIM,
                                        dv=B_HEAD_DIM, tq=512, cast_w=cast)
            bias = jnp.broadcast_to(a_bs[i].T[:, :, None], (CHUNK, A_GROUPS, A_GROUP_DIM)).reshape(CHUNK, A_WIDTH)
            X, f = _ab_out(u, vg, o, a_ws[i].astype(BF16), bias, ab_w_out[i].astype(BF16), X, mod,
                           g_post_mix[layer], g_pre_ffn[layer], S)
            X = _ffn(f, ffn_w_in[i].astype(BF16), ffn_w_out[i].astype(BF16), X, mod, g_post_ffn[layer], S)
        else:
            hw = 2 * LANES
            wuq = mla_w_uq[i].reshape(-1, C_HEADS, C_NOPE + C_ROPE)
            wuq = jnp.pad(wuq, ((0, 0), (0, 0), (0, hw - C_NOPE - C_ROPE))).reshape(-1, C_HEADS * hw)
            wdkv = jnp.pad(mla_w_dkv[i], ((0, 0), (0, LANES - C_ROPE)))
            wukv = mla_w_ukv[i].reshape(C_KV_RANK, C_HEADS, C_NOPE + C_V)
            wuk = wukv[:, :, :C_NOPE].reshape(C_KV_RANK, C_HEADS * C_NOPE)
            wuv = wukv[:, :, C_NOPE:].reshape(C_KV_RANK, C_HEADS * C_V)
            q, k, v = _mla_in(X, mod, g_pre_mix[layer], mla_w_dq[i].astype(BF16), mla_qnorm_g[i],
                              wuq.astype(BF16), wdkv.astype(BF16), mla_kvnorm_g[i], wuk.astype(BF16),
                              wuv.astype(BF16), tables_c, S)
            n_sorted = _n_expert_tiles(B * n_rows, moe_router.shape[-1]) * EXPERT_TILE
            o, moe_w_in_b, xs_zeros = _attention(q, k, v, n_lat=S, n_ctx=NC, need_ctx=need_ctx, name="mla",
                                                 n_kv_heads=C_HEADS, group=1, dk=hw, dv=C_V, tq=1024,
                                                 cast_w=(moe_w_in, i), zeros_shape=(n_sorted, D))
            X, f, logits_t = _mla_out(o, mla_w_o[i].astype(BF16), X, mod, g_post_mix[layer], g_pre_ffn[layer],
                                      moe_router[i].T, S, n_rows)
            X = _moe(f, logits_t, moe_w_in_b, moe_w_out_b, xs_zeros, X, mod, g_post_ffn[layer], S)
    return X[:, :S] if X.shape[1] != S else X
```
